```python
import jax, jax.numpy as jnp
from jax import lax
import numpy as np

D_MODEL = 1024
BATCH = 32
SEQ = 2048
DEPTH = 1

ATT_HEADS = 8
ATT_HEAD_DIM = 64
MOBA_BLOCK = 256
MOBA_TOPK = 3
MOBA_Q_CHUNK = 128
DN_HEADS = 8
DN_HEAD_DIM = 64
DN_CHUNK = 64
CONV_WIDTH = 4
D_FF = 4 * D_MODEL
NORM_EPS = 1e-6

ATT_W = ATT_HEADS * ATT_HEAD_DIM
DN_W = DN_HEADS * DN_HEAD_DIM
IN_SPLITS = (ATT_W, ATT_W, ATT_W, 3 * DN_W, DN_W, DN_HEADS, DN_HEADS, D_MODEL, D_MODEL)
IN_COLS = sum(IN_SPLITS)

kernel_name = "hybrid_moba_gdn_sqrelu_block"


def rms_norm(x, w):
    xf = x.astype(jnp.float32)
    y = xf * lax.rsqrt(jnp.mean(xf * xf, axis=-1, keepdims=True) + NORM_EPS)
    return (y * w.astype(jnp.float32)).astype(x.dtype)


def l2_norm(x):
    xf = x.astype(jnp.float32)
    return xf * lax.rsqrt(jnp.sum(xf * xf, axis=-1, keepdims=True) + NORM_EPS)


def causal_depthwise_conv(x, w):
    c = x.shape[-1]
    return lax.conv_general_dilated(
        x, w[:, None, :].astype(x.dtype), window_strides=(1,),
        padding=[(CONV_WIDTH - 1, 0)], dimension_numbers=("NWC", "WIO", "NWC"),
        feature_group_count=c)


def moba_attention(q, k, v):
    bsz, seq, h, d = q.shape
    nb = -(-seq // MOBA_BLOCK)
    n_sel = min(MOBA_TOPK, nb)
    nc = seq // MOBA_Q_CHUNK
    pad = nb * MOBA_BLOCK - seq
    scale = d ** -0.5
    neg = jnp.finfo(jnp.float32).min

    qh = q.transpose(0, 2, 1, 3)
    kb = jnp.pad(k.transpose(0, 2, 1, 3), ((0, 0), (0, 0), (0, pad), (0, 0)))
    vb = jnp.pad(v.transpose(0, 2, 1, 3), ((0, 0), (0, 0), (0, pad), (0, 0)))
    kb = kb.reshape(bsz, h, nb, MOBA_BLOCK, d)
    vb = vb.reshape(bsz, h, nb, MOBA_BLOCK, d)

    k_mean = jnp.mean(kb.astype(jnp.float32), axis=3)
    pos = jnp.arange(seq)
    blk_scores = jnp.einsum("bhsd,bhnd->bhsn", qh.astype(jnp.float32), k_mean)
    past = jnp.arange(nb)[None, :] < (pos // MOBA_BLOCK)[:, None]
    blk_scores = jnp.where(past, blk_scores, neg)
    _, sel_idx = lax.top_k(blk_scores, n_sel)

    def per_batch(args):
        q_b, kb_b, vb_b, idx_b = args
        q_c = q_b.reshape(h, nc, MOBA_Q_CHUNK, d).transpose(1, 0, 2, 3)
        idx_c = idx_b.reshape(h, nc, MOBA_Q_CHUNK, n_sel).transpose(1, 0, 2, 3)

        def per_chunk(cargs):
            c, qq, ii = cargs
            qpos = c * MOBA_Q_CHUNK + jnp.arange(MOBA_Q_CHUNK)
            own = (c * MOBA_Q_CHUNK) // MOBA_BLOCK
            k_sel = jax.vmap(lambda a, i: a[i])(kb_b, ii)
            v_sel = jax.vmap(lambda a, i: a[i])(vb_b, ii)
            s_sel = jnp.einsum("hcd,hcnkd->hcnk", qq, k_sel).astype(jnp.float32) * scale
            valid = jnp.arange(n_sel)[None, :] < (qpos // MOBA_BLOCK)[:, None]
            s_sel = jnp.where(valid[None, :, :, None], s_sel, neg)
            k_own = lax.dynamic_index_in_dim(kb_b, own, axis=1, keepdims=False)
            v_own = lax.dynamic_index_in_dim(vb_b, own, axis=1, keepdims=False)
            s_own = jnp.einsum("hcd,hkd->hck", qq, k_own).astype(jnp.float32) * scale
            kpos = own * MOBA_BLOCK + jnp.arange(MOBA_BLOCK)
            s_own = jnp.where(kpos[None, :] <= qpos[:, None], s_own, neg)
            scores = jnp.concatenate([s_sel.reshape(h, MOBA_Q_CHUNK, n_sel * MOBA_BLOCK), s_own], axis=-1)
            p = jax.nn.softmax(scores, axis=-1).astype(qq.dtype)
            p_sel = p[..., : n_sel * MOBA_BLOCK].reshape(h, MOBA_Q_CHUNK, n_sel, MOBA_BLOCK)
            p_own = p[..., n_sel * MOBA_BLOCK:]
            return (jnp.einsum("hcnk,hcnkd->hcd", p_sel, v_sel)
                    + jnp.einsum("hck,hkd->hcd", p_own, v_own))

        out = lax.map(per_chunk, (jnp.arange(nc), q_c, idx_c))
        return out.transpose(1, 0, 2, 3).reshape(h, seq, d)

    out = lax.map(per_batch, (qh, kb, vb, sel_idx))
    return out.transpose(0, 2, 1, 3).reshape(bsz, seq, h * d)


def gated_delta_rule(q, k, v, g, beta):
    bsz, seq, h, dk = q.shape
    dv = v.shape[-1]
    n = seq // DN_CHUNK
    f32 = jnp.float32

    def chunks(x):
        x = jnp.moveaxis(x.astype(f32), 2, 1)
        return x.reshape((bsz, h, n, DN_CHUNK) + x.shape[3:])

    qc = chunks(l2_norm(q)) * (dk ** -0.5)
    kc = chunks(l2_norm(k))
    vc = chunks(v)
    bc = chunks(beta)
    gc = jnp.cumsum(chunks(g), axis=-1)

    tri = jnp.tril(jnp.ones((DN_CHUNK, DN_CHUNK), bool))
    strict = jnp.tril(jnp.ones((DN_CHUNK, DN_CHUNK), bool), -1)
    diff = gc[..., :, None] - gc[..., None, :]
    decay = jnp.where(tri, jnp.exp(jnp.where(tri, diff, 0.0)), 0.0)

    k_beta = kc * bc[..., None]
    lower = jnp.where(strict, jnp.einsum("bhnid,bhnjd->bhnij", k_beta, kc) * decay, 0.0)
    a_mat = jnp.eye(DN_CHUNK, dtype=f32) + lower
    rhs = jnp.concatenate([vc * bc[..., None], k_beta * jnp.exp(gc)[..., None]], axis=-1)
    sol = lax.linalg.triangular_solve(a_mat, rhs, left_side=True, lower=True)
    u, w = sol[..., :dv], sol[..., dv:]

    qk = jnp.where(tri, jnp.einsum("bhnid,bhnjd->bhnij", qc, kc) * decay, 0.0)
    q_dec = qc * jnp.exp(gc)[..., None]
    k_dec = kc * jnp.exp(gc[..., -1:] - gc)[..., None]
    g_last = jnp.exp(gc[..., -1])

    def step(state, xs):
        q_i, k_i, u_i, w_i, qk_i, gl_i = xs
        v_new = u_i - jnp.einsum("bhck,bhkv->bhcv", w_i, state)
        o = jnp.einsum("bhck,bhkv->bhcv", q_i, state) + jnp.einsum("bhij,bhjv->bhiv", qk_i, v_new)
        state = state * gl_i[..., None, None] + jnp.einsum("bhck,bhcv->bhkv", k_i, v_new)
        return state, o

    xs = (jnp.moveaxis(q_dec, 2, 0), jnp.moveaxis(k_dec, 2, 0), jnp.moveaxis(u, 2, 0),
          jnp.moveaxis(w, 2, 0), jnp.moveaxis(qk, 2, 0), jnp.moveaxis(g_last, 2, 0))
    state0 = jnp.zeros((bsz, h, dk, dv), f32)
    _, o = lax.scan(step, state0, xs)
    o = jnp.moveaxis(o, 0, 2).reshape(bsz, h, seq, dv)
    return jnp.moveaxis(o, 1, 2)


def hybrid_mixer(h, w_in, conv_w, a_log, dt_bias, dn_norm, w_branch_att, w_branch_dn, w_out):
    bsz, seq, _ = h.shape
    proj = h @ w_in
    split_at = [int(s) for s in np.cumsum(IN_SPLITS)[:-1]]
    qa, ka, va, qkv_dn, z_dn, b_dn, a_dn, gate_att, gate_dn = jnp.split(proj, split_at, axis=-1)

    shp_a = (bsz, seq, ATT_HEADS, ATT_HEAD_DIM)
    y_att = moba_attention(qa.reshape(shp_a), ka.reshape(shp_a), va.reshape(shp_a))

    qkv_dn = jax.nn.silu(causal_depthwise_conv(qkv_dn, conv_w))
    qd, kd, vd = jnp.split(qkv_dn, [DN_W, 2 * DN_W], axis=-1)
    shp_b = (bsz, seq, DN_HEADS, DN_HEAD_DIM)
    beta = jax.nn.sigmoid(b_dn.astype(jnp.float32))
    g = -jnp.exp(a_log.astype(jnp.float32)) * jax.nn.softplus(
        a_dn.astype(jnp.float32) + dt_bias.astype(jnp.float32))
    o_dn = gated_delta_rule(qd.reshape(shp_b), kd.reshape(shp_b), vd.reshape(shp_b), g, beta)
    o_dn = rms_norm(o_dn, dn_norm) * jax.nn.silu(z_dn.reshape(shp_b).astype(jnp.float32))
    y_dn = o_dn.reshape(bsz, seq, DN_W).astype(h.dtype)

    merged = (jax.nn.sigmoid(gate_att) * (y_att @ w_branch_att)
              + jax.nn.sigmoid(gate_dn) * (y_dn @ w_branch_dn))
    return merged @ w_out


def setup_inputs(seed: int = 0) -> dict:
    key = jax.random.key(seed)
    ks = jax.random.split(key, 16)
    f32 = jnp.float32

    def dense(k, fan_in, fan_out):
        return jax.random.normal(k, (DEPTH, fan_in, fan_out), f32) * fan_in ** -0.5

    def gain(k, n):
        return 1.0 + 0.05 * jax.random.normal(k, (DEPTH, n), f32)

    dt = jnp.exp(jax.random.uniform(ks[5], (DEPTH, DN_HEADS), f32, np.log(1e-3), np.log(1e-1)))
    return {
        "x": jax.random.normal(ks[0], (BATCH, SEQ, D_MODEL), f32),
        "pre_norm_mix": gain(ks[1], D_MODEL),
        "w_in": dense(ks[2], D_MODEL, IN_COLS),
        "conv_w": jax.random.normal(ks[3], (DEPTH, CONV_WIDTH, 3 * DN_W), f32) * CONV_WIDTH ** -0.5,
        "a_log": jnp.log(jax.random.uniform(ks[4], (DEPTH, DN_HEADS), f32, 1.0, 16.0)),
        "dt_bias": jnp.log(jnp.expm1(dt)),
        "dn_norm": gain(ks[6], DN_HEAD_DIM),
        "w_branch_att": dense(ks[7], ATT_W, D_MODEL),
        "w_branch_dn": dense(ks[8], DN_W, D_MODEL),
        "w_out": dense(ks[9], D_MODEL, D_MODEL),
        "post_norm_mix": gain(ks[10], D_MODEL),
        "pre_norm_mlp": gain(ks[11], D_MODEL),
        "w_mlp_in": dense(ks[12], D_MODEL, D_FF),
        "w_mlp_out": dense(ks[13], D_FF, D_MODEL),
        "post_norm_mlp": gain(ks[14], D_MODEL),
    }


def reference(x, pre_norm_mix, w_in, conv_w, a_log, dt_bias, dn_norm, w_branch_att, w_branch_dn,
              w_out, post_norm_mix, pre_norm_mlp, w_mlp_in, w_mlp_out, post_norm_mlp):
    for l in range(DEPTH):
        h = rms_norm(x, pre_norm_mix[l])
        y = hybrid_mixer(h, w_in[l], conv_w[l], a_log[l], dt_bias[l], dn_norm[l],
                         w_branch_att[l], w_branch_dn[l], w_out[l])
        x = x + rms_norm(y, post_norm_mix[l])
        h = rms_norm(x, pre_norm_mlp[l])
        y = jnp.square(jax.nn.relu(h @ w_mlp_in[l])) @ w_mlp_out[l]
        x = x + rms_norm(y, post_norm_mlp[l])
    return x
```

```python
import functools

import jax
import jax.numpy as jnp
from jax import lax
from jax.experimental import pallas as pl
from jax.experimental.pallas import tpu as pltpu

ATT_HEADS = 8
DN_HEADS = 8
HEAD_DIM = 64
MOBA_BLOCK = 256
MOBA_TOPK = 3
DN_CHUNK = 64
CONV_WIDTH = 4
NORM_EPS = 1e-6

LANES = 128
V7X_VMEM_BYTES = 64 * 1024 * 1024
VMEM_CAP_BYTES = 56 * 1024 * 1024

F32 = jnp.float32
BF16 = jnp.bfloat16
NEG_BIG = -1e30

_NT = (((1,), (1,)), ((), ()))
_TN = (((0,), (0,)), ((), ()))


def _vmem_limit(nbytes):
    return int(min(VMEM_CAP_BYTES, nbytes * 5 // 4 + (4 << 20)))


def _dot(a, b):
    return jnp.dot(a, b, preferred_element_type=F32)


def _rms(x, w):
    return x * lax.rsqrt(jnp.mean(x * x, axis=-1, keepdims=True) + NORM_EPS) * w


def _sigmoid(x):
    return 1.0 / (1.0 + jnp.exp(-x))


def _split3(x):
    hi = x.astype(BF16)
    r = x - hi.astype(F32)
    mid = r.astype(BF16)
    lo = (r - mid.astype(F32)).astype(BF16)
    return hi, mid, lo


def _inproj_kernel(x_ref, g_ref, wm_ref, ws_ref, att_ref, dn_ref, z_ref, gate_ref, small_ref, *, col_chunk):
    hb = _rms(x_ref[...], g_ref[...]).astype(BF16)
    start = 0
    for ref in (att_ref, dn_ref, z_ref, gate_ref):
        width = ref.shape[1]
        for c in range(0, width, col_chunk):
            ref[:, c:c + col_chunk] = _dot(hb, wm_ref[:, start + c:start + c + col_chunk]).astype(ref.dtype)
        start += width
    small_ref[...] = _dot(hb, ws_ref[...])


def _inproj(x2, gain, w_main, w_small, *, tm):
    t, d = x2.shape
    att_w, dn_w, z_w, gate_w = 3 * ATT_HEADS * HEAD_DIM, 3 * DN_HEADS * HEAD_DIM, DN_HEADS * HEAD_DIM, 2 * d
    assert w_main.shape == (d, att_w + dn_w + z_w + gate_w) and t % tm == 0
    row = lambda w: pl.BlockSpec((tm, w), lambda i: (i, 0))
    const = lambda shp: pl.BlockSpec(shp, lambda i: (0, 0))
    est = 2 * (tm * d * 4 + w_main.size * 2 + w_small.size * 2 + tm * w_main.shape[1] * 2 + tm * LANES * 4) + 4 * tm * d * 4
    return pl.pallas_call(
        functools.partial(_inproj_kernel, col_chunk=512),
        grid=(t // tm,),
        in_specs=[row(d), const((1, d)), const(w_main.shape), const(w_small.shape)],
        out_specs=[row(att_w), row(dn_w), row(z_w), row(gate_w), row(LANES)],
        out_shape=[jax.ShapeDtypeStruct((t, att_w), BF16), jax.ShapeDtypeStruct((t, dn_w), BF16),
                   jax.ShapeDtypeStruct((t, z_w), BF16), jax.ShapeDtypeStruct((t, gate_w), BF16),
                   jax.ShapeDtypeStruct((t, LANES), F32)],
        compiler_params=pltpu.CompilerParams(dimension_semantics=("arbitrary",), vmem_limit_bytes=_vmem_limit(est)),
        name="inproj",
    )(x2, gain, w_main, w_small)


def _moba_kernel(q_ref, k_ref, v_ref, o_ref, kg_hi_ref, kg_lo_ref, m_ref, l_ref, acc_ref, sel_ref, *, nb):
    blk = MOBA_BLOCK
    qi = pl.program_id(2)
    lane = lax.broadcasted_iota(jnp.int32, (blk, LANES), 1)
    first_head = lane < HEAD_DIM

    @pl.when(qi == 0)
    def _():
        km = jnp.concatenate(
            [jnp.sum(k_ref[0, n * blk:(n + 1) * blk, :].astype(F32), axis=0, keepdims=True) for n in range(nb)]
            + [jnp.zeros((8 - nb, LANES), F32)] * (1 if nb < 8 else 0), axis=0) * (1.0 / blk)
        l8 = lax.broadcasted_iota(jnp.int32, (8, LANES), 1) < HEAD_DIM
        g = jnp.concatenate([jnp.where(l8, km, 0.0), jnp.where(l8, 0.0, km), jnp.zeros((LANES - 16, LANES), F32)], axis=0)
        hi = g.astype(BF16)
        kg_hi_ref[...] = hi
        kg_lo_ref[...] = (g - hi.astype(F32)).astype(BF16)

    q2 = q_ref[0]

    s_t = (lax.dot_general(kg_hi_ref[...], q2, _NT, preferred_element_type=F32)
           + lax.dot_general(kg_lo_ref[...], q2, _NT, preferred_element_type=F32))
    row = lax.broadcasted_iota(jnp.int32, (8, blk), 0)
    past = row < qi

    def select(s):
        out = jnp.zeros((8, blk), F32)
        for n in range(nb):
            rn = s[n:n + 1, :]
            beats = jnp.where(s > rn, 1.0, jnp.where(jnp.logical_and(s == rn, row < n), 1.0, 0.0))
            cnt = jnp.sum(jnp.where(past, beats, 0.0), axis=0, keepdims=True)
            out = jnp.where(row == n, jnp.where(cnt < float(MOBA_TOPK), 1.0, 0.0), out)
        return jnp.where(past, out, 0.0)

    sel_t = jnp.concatenate([select(s_t[0:8]), select(s_t[8:16]), jnp.zeros((LANES - 16, blk), F32)], axis=0)
    sel_ref[...] = sel_t.T

    scale = HEAD_DIM ** -0.5
    zero = jnp.zeros_like(q2)
    q_heads = (jnp.where(first_head, q2, zero) * scale, jnp.where(first_head, zero, q2) * scale)

    def attend(kblk, vblk, masks, init):
        pvs, alphas = [], []
        for hd in range(2):
            s = lax.dot_general(q_heads[hd], kblk, _NT, preferred_element_type=F32)
            s = jnp.where(masks[hd], s, NEG_BIG)
            m_prev = jnp.full((blk, LANES), NEG_BIG, F32) if init else m_ref[hd]
            m_new = jnp.maximum(m_prev, jnp.max(s, axis=1, keepdims=True))
            alpha = jnp.exp(m_prev - m_new)
            p = jnp.exp(s - jnp.concatenate([m_new, m_new], axis=1))
            psum = jnp.sum(p, axis=1, keepdims=True)
            l_ref[hd] = psum + (jnp.zeros((blk, LANES), F32) if init else alpha * l_ref[hd])
            m_ref[hd] = m_new
            pvs.append(_dot(p.astype(BF16), vblk))
            alphas.append(alpha)
        pv = jnp.where(first_head, pvs[0], pvs[1])
        if init:
            acc_ref[...] = pv
        else:
            acc_ref[...] = acc_ref[...] * jnp.where(first_head, alphas[0], alphas[1]) + pv

    own = pl.multiple_of(qi * blk, blk)
    tri = lax.broadcasted_iota(jnp.int32, (blk, blk), 1) <= lax.broadcasted_iota(jnp.int32, (blk, blk), 0)
    attend(k_ref[0, pl.ds(own, blk), :], v_ref[0, pl.ds(own, blk), :], (tri, tri), True)

    for n in range(nb - 1):
        @pl.when(n < qi)
        def _(n=n):
            sel = sel_ref[...]
            masks = (sel[:, n:n + 1] > 0.5, sel[:, 8 + n:9 + n] > 0.5)
            attend(k_ref[0, n * blk:(n + 1) * blk, :], v_ref[0, n * blk:(n + 1) * blk, :], masks, False)

    inv = jnp.where(first_head, 1.0 / l_ref[0], 1.0 / l_ref[1])
    o_ref[0] = (acc_ref[...] * inv).astype(o_ref.dtype)


def _moba(qkv):
    b, s, w3 = qkv.shape
    blk = MOBA_BLOCK
    npair = ATT_HEADS // 2
    assert w3 == 3 * ATT_HEADS * HEAD_DIM and s % blk == 0 and s // blk <= 8
    nb = s // blk
    est = 2 * (2 * blk * LANES * 2 + 2 * s * LANES * 2) + 6 * blk * LANES * 4 + 8 * blk * blk * 4
    return pl.pallas_call(
        functools.partial(_moba_kernel, nb=nb),
        grid=(b, npair, nb),
        in_specs=[pl.BlockSpec((1, blk, LANES), lambda i, p, j: (i, j, p)),
                  pl.BlockSpec((1, s, LANES), lambda i, p, j: (i, 0, npair + p)),
                  pl.BlockSpec((1, s, LANES), lambda i, p, j: (i, 0, 2 * npair + p))],
        out_specs=pl.BlockSpec((1, blk, LANES), lambda i, p, j: (i, j, p)),
        out_shape=jax.ShapeDtypeStruct((b, s, ATT_HEADS * HEAD_DIM), BF16),
        scratch_shapes=[pltpu.VMEM((LANES, LANES), BF16), pltpu.VMEM((LANES, LANES), BF16),
                        pltpu.VMEM((2, blk, LANES), F32), pltpu.VMEM((2, blk, LANES), F32),
                        pltpu.VMEM((blk, LANES), F32), pltpu.VMEM((blk, LANES), F32)],
        compiler_params=pltpu.CompilerParams(dimension_semantics=("arbitrary", "arbitrary", "arbitrary"),
                                             vmem_limit_bytes=_vmem_limit(est)),
        name="moba",
    )(qkv, qkv, qkv)


def _gdn_kernel(x_ref, sm_ref, cw_ref, alog_ref, dtb_ref, ltri_ref, exp_ref, hsum_ref, o_ref, xs_ref, st_ref, *, rows):
    ck = DN_CHUNK
    dn_w = DN_HEADS * HEAD_DIM
    npair = DN_HEADS // 2
    c = pl.program_id(1)
    halo = 8

    @pl.when(c == 0)
    def _():
        xs_ref[0:halo, :] = jnp.zeros((halo, 3 * dn_w), F32)
        st_ref[...] = jnp.zeros_like(st_ref)

    xs_ref[halo:halo + rows, :] = x_ref[0].astype(F32)
    cw = cw_ref[...]
    y = xs_ref[halo - 3:halo - 3 + rows, :] * cw[0:1, :]
    for j in range(1, CONV_WIDTH):
        y = y + xs_ref[halo - 3 + j:halo - 3 + j + rows, :] * cw[j:j + 1, :]
    xs_ref[0:halo, :] = xs_ref[rows:rows + halo, :]
    y = y * _sigmoid(y)
    q, k, v = y[:, :dn_w], y[:, dn_w:2 * dn_w], y[:, 2 * dn_w:]

    hsum = hsum_ref[...]

    def head_sumsq(a):
        sq = a * a
        hi = sq.astype(BF16)
        lo = (sq - hi.astype(F32)).astype(BF16)
        return _dot(hi, hsum) + _dot(lo, hsum)

    qn = q * (lax.rsqrt(head_sumsq(q) + NORM_EPS) * (HEAD_DIM ** -0.5))
    kn = k * lax.rsqrt(head_sumsq(k) + NORM_EPS)

    sm = sm_ref[0]
    lane = lax.broadcasted_iota(jnp.int32, (rows, LANES), 1)
    xa = sm + dtb_ref[...]
    softplus = jnp.maximum(xa, 0.0) + jnp.log(1.0 + jnp.exp(-jnp.abs(xa)))
    g = -jnp.exp(alog_ref[...]) * softplus
    ltri = ltri_ref[...]
    gc = sum(_dot(ltri, t) for t in _split3(g))
    comb = jnp.where(lane < DN_HEADS, _sigmoid(sm), gc)
    expd = exp_ref[...]
    ex = sum(_dot(t, expd) for t in _split3(comb))
    bexp, gexp = ex[:, :dn_w], ex[:, dn_w:]

    eg = jnp.exp(gexp)
    vb = v * bexp
    kb = kn * bexp
    kbe = kb * eg
    qd = qn * eg

    ri = lax.broadcasted_iota(jnp.int32, (ck, dn_w), 0)
    ci = jnp.bitwise_and(lax.broadcasted_iota(jnp.int32, (ck, dn_w), 1), ck - 1)
    tri, strict, diag = ri >= ci, ri > ci, ri == ci
    r2 = lax.broadcasted_iota(jnp.int32, (LANES, LANES), 0) // HEAD_DIM
    c2 = lax.broadcasted_iota(jnp.int32, (LANES, LANES), 1) // HEAD_DIM
    bmask = r2 == c2
    eye2 = jnp.where(diag[:, :LANES], 1.0, 0.0)

    def bd2(a):
        return jnp.where(bmask, jnp.concatenate([a, a], axis=0), 0.0).astype(BF16)

    for cc in range(rows // ck):
        sl = slice(cc * ck, (cc + 1) * ck)
        gch = gexp[sl]
        grow = jnp.sum(jnp.where(diag, gch, 0.0), axis=0, keepdims=True)
        dec = jnp.where(tri, jnp.exp(jnp.where(tri, gch - grow, 0.0)), 0.0)
        glast = gch[ck - 1:ck, :]
        kdec = kn[sl] * jnp.exp(glast - gch)
        gl_exp = jnp.exp(glast)
        for p in range(npair):
            ps = slice(p * LANES, (p + 1) * LANES)
            lhs = jnp.concatenate([kb[sl, ps], qn[sl, ps]], axis=0).astype(BF16)
            kq = lax.dot_general(lhs, bd2(kn[sl, ps]), _NT, preferred_element_type=F32)
            decp = dec[:, ps]
            neg_l = -jnp.where(strict[:, ps], kq[:ck] * decp, 0.0)
            qk = jnp.where(tri[:, ps], kq[ck:] * decp, 0.0)
            ssum = eye2 + neg_l
            pw = _dot(neg_l.astype(BF16), bd2(neg_l))
            span = 2
            while span * 2 < ck:
                both = _dot(jnp.concatenate([ssum, pw], axis=0).astype(BF16), bd2(pw))
                ssum, pw = ssum + both[:ck], both[ck:]
                span *= 2
            tinv = ssum + _dot(ssum.astype(BF16), bd2(pw))
            uw = _dot(tinv.astype(BF16), jnp.concatenate([bd2(vb[sl, ps]), bd2(kbe[sl, ps])], axis=1))
            u, w = uw[:, :LANES], uw[:, LANES:]
            state = st_ref[p]
            wq = _dot(jnp.concatenate([w, qd[sl, ps]], axis=0).astype(BF16), state.astype(BF16))
            vnew = u - wq[:ck]
            o_ref[0, sl, ps] = (wq[ck:] + _dot(qk.astype(BF16), bd2(vnew))).astype(o_ref.dtype)
            upd = lax.dot_general(kdec[:, ps].astype(BF16), vnew.astype(BF16), _TN, preferred_element_type=F32)
            st_ref[p] = state * gl_exp[:, ps] + jnp.where(bmask, upd, 0.0)


def _gdn(qkv_dn, small, conv_w, a_log, dt_bias, *, rows):
    b, s, w3 = qkv_dn.shape
    dn_w = DN_HEADS * HEAD_DIM
    assert w3 == 3 * dn_w and s % rows == 0 and rows % DN_CHUNK == 0
    h = DN_HEADS
    alog_row = jnp.zeros((1, LANES), F32).at[0, h:2 * h].set(a_log.astype(F32))
    dtb_row = jnp.zeros((1, LANES), F32).at[0, h:2 * h].set(dt_bias.astype(F32))
    r = jnp.arange(rows)
    ltri = ((r[:, None] // DN_CHUNK == r[None, :] // DN_CHUNK) & (r[None, :] <= r[:, None])).astype(BF16)
    src = jnp.arange(LANES)[:, None]
    dst = jnp.arange(2 * dn_w)[None, :]
    expander = ((src < 2 * h) & (dst // dn_w == src // h) & ((dst % dn_w) // HEAD_DIM == src % h)).astype(BF16)
    hl = jnp.arange(dn_w) // HEAD_DIM
    hsum = (hl[:, None] == hl[None, :]).astype(BF16)
    const = lambda a: pl.BlockSpec(a.shape, lambda i, j: (0,) * a.ndim)
    est = (2 * (rows * w3 * 2 + rows * LANES * 4 + rows * dn_w * 4) + (rows + 8) * w3 * 4 + 12 * rows * w3 * 4
           + 2 * (ltri.size + expander.size + hsum.size) * 2)
    return pl.pallas_call(
        functools.partial(_gdn_kernel, rows=rows),
        grid=(b, s // rows),
        in_specs=[pl.BlockSpec((1, rows, w3), lambda i, j: (i, j, 0)),
                  pl.BlockSpec((1, rows, LANES), lambda i, j: (i, j, 0)),
                  const(conv_w), const(alog_row), const(dtb_row), const(ltri), const(expander), const(hsum)],
        out_specs=pl.BlockSpec((1, rows, dn_w), lambda i, j: (i, j, 0)),
        out_shape=jax.ShapeDtypeStruct((b, s, dn_w), F32),
        scratch_shapes=[pltpu.VMEM((rows + 8, w3), F32), pltpu.VMEM((DN_HEADS // 2, LANES, LANES), F32)],
        compiler_params=pltpu.CompilerParams(dimension_semantics=("arbitrary", "arbitrary"),
                                             vmem_limit_bytes=_vmem_limit(est)),
        name="gdn",
    )(qkv_dn, small, conv_w.astype(F32), alog_row, dtb_row, ltri, expander, hsum)


def _mixout_kernel(x_ref, ya_ref, od_ref, z_ref, gate_ref, dnw_ref, hmean_ref, wa_ref, wd_ref, wo_ref, pn_ref, o_ref):
    d = x_ref.shape[1]
    od = od_ref[...]
    sq = od * od
    hi = sq.astype(BF16)
    lo = (sq - hi.astype(F32)).astype(BF16)
    ms = _dot(hi, hmean_ref[...]) + _dot(lo, hmean_ref[...])
    z = z_ref[...].astype(F32)
    y_dn = od * lax.rsqrt(ms + NORM_EPS) * dnw_ref[...] * (z * _sigmoid(z))
    ga = _sigmoid(gate_ref[:, :d].astype(F32))
    gd = _sigmoid(gate_ref[:, d:].astype(F32))
    merged = ga * _dot(ya_ref[...], wa_ref[...]) + gd * _dot(y_dn.astype(BF16), wd_ref[...])
    y = _dot(merged.astype(BF16), wo_ref[...])
    o_ref[...] = x_ref[...] + _rms(y, pn_ref[...])


def _mixout(x2, y_att, o_dn, z, gates, dn_norm, wa, wd, wo, post_norm, *, tm):
    t, d = x2.shape
    dn_w = DN_HEADS * HEAD_DIM
    dnw_row = jnp.tile(dn_norm.astype(F32), DN_HEADS)[None, :]
    hl = jnp.arange(dn_w) // HEAD_DIM
    hmean = ((hl[:, None] == hl[None, :]).astype(F32) / HEAD_DIM).astype(BF16)
    row = lambda w: pl.BlockSpec((tm, w), lambda i: (i, 0))
    const = lambda a: pl.BlockSpec(a.shape, lambda i: (0, 0))
    est = (2 * (2 * tm * d * 4 + tm * dn_w * (2 + 4 + 2) + tm * 2 * d * 2)
           + 2 * (hmean.size + wa.size + wd.size + wo.size) * 2 + 8 * tm * d * 4)
    return pl.pallas_call(
        _mixout_kernel,
        grid=(t // tm,),
        in_specs=[row(d), row(dn_w), row(dn_w), row(dn_w), row(2 * d), const(dnw_row), const(hmean),
                  const(wa), const(wd), const(wo), const(post_norm)],
        out_specs=row(d),
        out_shape=jax.ShapeDtypeStruct((t, d), F32),
        compiler_params=pltpu.CompilerParams(dimension_semantics=("arbitrary",), vmem_limit_bytes=_vmem_limit(est)),
        name="mixout",
    )(x2, y_att, o_dn, z, gates, dnw_row, hmean, wa, wd, wo, post_norm)


def _mlp_kernel(x_ref, pre_ref, w1_ref, w2_ref, post_ref, o_ref, *, ff_chunk):
    x = x_ref[...]
    hb = _rms(x, pre_ref[...]).astype(BF16)
    acc = jnp.zeros(x.shape, F32)
    for c in range(0, w1_ref.shape[1], ff_chunk):
        a = jnp.maximum(_dot(hb, w1_ref[:, c:c + ff_chunk]), 0.0)
        acc = acc + _dot((a * a).astype(BF16), w2_ref[c:c + ff_chunk, :])
    o_ref[...] = x + _rms(acc, post_ref[...])


def _mlp(x1, pre, w1, w2, post, *, tm):
    t, d = x1.shape
    row = pl.BlockSpec((tm, d), lambda i: (i, 0))
    const = lambda a: pl.BlockSpec(a.shape, lambda i: (0, 0))
    est = 2 * (2 * tm * d * 4 + (w1.size + w2.size) * 2) + 6 * tm * d * 4 + 2 * tm * 1024 * 4
    return pl.pallas_call(
        functools.partial(_mlp_kernel, ff_chunk=1024),
        grid=(t // tm,),
        in_specs=[row, const(pre), const(w1), const(w2), const(post)],
        out_specs=row,
        out_shape=jax.ShapeDtypeStruct((t, d), F32),
        compiler_params=pltpu.CompilerParams(dimension_semantics=("arbitrary",), vmem_limit_bytes=_vmem_limit(est)),
        name="mlp",
    )(x1, pre, w1, w2, post)


def kernel(x, pre_norm_mix, w_in, conv_w, a_log, dt_bias, dn_norm, w_branch_att, w_branch_dn,
           w_out, post_norm_mix, pre_norm_mlp, w_mlp_in, w_mlp_out, post_norm_mlp):
    b, s, d = x.shape
    att_w = ATT_HEADS * HEAD_DIM
    dn_w = DN_HEADS * HEAD_DIM
    n_main = 3 * att_w + 3 * dn_w + dn_w
    tm = 512
    x2 = x.reshape(b * s, d)
    for l in range(w_in.shape[0]):
        wl = w_in[l]
        w_main = jnp.concatenate([wl[:, :n_main], wl[:, n_main + 2 * DN_HEADS:]], axis=1).astype(BF16)
        w_small = jnp.pad(wl[:, n_main:n_main + 2 * DN_HEADS], ((0, 0), (0, LANES - 2 * DN_HEADS))).astype(BF16)
        qkv_att, qkv_dn, z, gates, small = _inproj(x2, pre_norm_mix[l][None, :], w_main, w_small, tm=tm)
        y_att = _moba(qkv_att.reshape(b, s, 3 * att_w))
        o_dn = _gdn(qkv_dn.reshape(b, s, 3 * dn_w), small.reshape(b, s, LANES), conv_w[l], a_log[l], dt_bias[l], rows=256)
        x2 = _mixout(x2, y_att.reshape(b * s, att_w), o_dn.reshape(b * s, dn_w), z, gates, dn_norm[l],
                     w_branch_att[l].astype(BF16), w_branch_dn[l].astype(BF16), w_out[l].astype(BF16),
                     post_norm_mix[l][None, :], tm=tm)
        x2 = _mlp(x2, pre_norm_mlp[l][None, :], w_mlp_in[l].astype(BF16), w_mlp_out[l].astype(BF16),
                  post_norm_mlp[l][None, :], tm=tm)
    return x2.reshape(b, s, d)
```

```python
import functools

import jax
import jax.numpy as jnp
from jax import lax
from jax.experimental import pallas as pl
from jax.experimental.pallas import tpu as pltpu

ATT_HEADS = 8
DN_HEADS = 8
HEAD_DIM = 64
MOBA_BLOCK = 256
MOBA_TOPK = 3
DN_CHUNK = 64
CONV_WIDTH = 4
NORM_EPS = 1e-6

LANES = 128
V7X_VMEM_BYTES = 64 * 1024 * 1024
VMEM_CAP_BYTES = 56 * 1024 * 1024

F32 = jnp.float32
BF16 = jnp.bfloat16
NEG_BIG = -1e30

_NT = (((1,), (1,)), ((), ()))
_TN = (((0,), (0,)), ((), ()))


def _vmem_limit(nbytes):
    return int(min(VMEM_CAP_BYTES, nbytes * 5 // 4 + (4 << 20)))


def _dot(a, b):
    return jnp.dot(a, b, preferred_element_type=F32)


def _rms(x, w):
    return x * lax.rsqrt(jnp.mean(x * x, axis=-1, keepdims=True) + NORM_EPS) * w


def _sigmoid(x):
    return 1.0 / (1.0 + jnp.exp(-x))


def _split3(x):
    hi = x.astype(BF16)
    r = x - hi.astype(F32)
    mid = r.astype(BF16)
    lo = (r - mid.astype(F32)).astype(BF16)
    return hi, mid, lo


def _inproj_kernel(x_ref, g_ref, wm_ref, ws_ref, att_ref, dn_ref, z_ref, gate_ref, small_ref, *, col_chunk):
    hb = _rms(x_ref[...], g_ref[...]).astype(BF16)
    start = 0
    for ref in (att_ref, dn_ref, z_ref, gate_ref):
        width = ref.shape[1]
        for c in range(0, width, col_chunk):
            ref[:, c:c + col_chunk] = _dot(hb, wm_ref[:, start + c:start + c + col_chunk]).astype(ref.dtype)
        start += width
    small_ref[...] = _dot(hb, ws_ref[...])


def _inproj(x2, gain, w_main, w_small, *, tm):
    t, d = x2.shape
    att_w, dn_w, z_w, gate_w = 3 * ATT_HEADS * HEAD_DIM, 3 * DN_HEADS * HEAD_DIM, DN_HEADS * HEAD_DIM, 2 * d
    assert w_main.shape == (d, att_w + dn_w + z_w + gate_w) and t % tm == 0
    row = lambda w: pl.BlockSpec((tm, w), lambda i: (i, 0))
    const = lambda shp: pl.BlockSpec(shp, lambda i: (0, 0))
    est = 2 * (tm * d * 4 + w_main.size * 2 + w_small.size * 2 + tm * w_main.shape[1] * 2 + tm * LANES * 4) + 4 * tm * d * 4
    return pl.pallas_call(
        functools.partial(_inproj_kernel, col_chunk=512),
        grid=(t // tm,),
        in_specs=[row(d), const((1, d)), const(w_main.shape), const(w_small.shape)],
        out_specs=[row(att_w), row(dn_w), row(z_w), row(gate_w), row(LANES)],
        out_shape=[jax.ShapeDtypeStruct((t, att_w), BF16), jax.ShapeDtypeStruct((t, dn_w), BF16),
                   jax.ShapeDtypeStruct((t, z_w), BF16), jax.ShapeDtypeStruct((t, gate_w), BF16),
                   jax.ShapeDtypeStruct((t, LANES), F32)],
        compiler_params=pltpu.CompilerParams(dimension_semantics=("arbitrary",), vmem_limit_bytes=_vmem_limit(est)),
        name="inproj",
    )(x2, gain, w_main, w_small)


def _moba_kernel(q_ref, k_ref, v_ref, o_ref, kg_hi_ref, kg_lo_ref, m_ref, l_ref, acc_ref, sel_ref, *, nb):
    blk = MOBA_BLOCK
    qi = pl.program_id(2)
    lane = lax.broadcasted_iota(jnp.int32, (blk, LANES), 1)
    first_head = lane < HEAD_DIM

    @pl.when(qi == 0)
    def _():
        km = jnp.concatenate(
            [jnp.sum(k_ref[0, n * blk:(n + 1) * blk, :].astype(F32), axis=0, keepdims=True) for n in range(nb)]
            + [jnp.zeros((8 - nb, LANES), F32)] * (1 if nb < 8 else 0), axis=0) * (1.0 / blk)
        l8 = lax.broadcasted_iota(jnp.int32, (8, LANES), 1) < HEAD_DIM
        g = jnp.concatenate([jnp.where(l8, km, 0.0), jnp.where(l8, 0.0, km), jnp.zeros((LANES - 16, LANES), F32)], axis=0)
        hi = g.astype(BF16)
        kg_hi_ref[...] = hi
        kg_lo_ref[...] = (g - hi.astype(F32)).astype(BF16)

    q2 = q_ref[0]

    s_t = (lax.dot_general(kg_hi_ref[...], q2, _NT, preferred_element_type=F32)
           + lax.dot_general(kg_lo_ref[...], q2, _NT, preferred_element_type=F32))
    row = lax.broadcasted_iota(jnp.int32, (8, blk), 0)
    past = row < qi

    def select(s):
        out = jnp.zeros((8, blk), F32)
        for n in range(nb):
            rn = s[n:n + 1, :]
            beats = jnp.where(s > rn, 1.0, jnp.where(jnp.logical_and(s == rn, row < n), 1.0, 0.0))
            cnt = jnp.sum(jnp.where(past, beats, 0.0), axis=0, keepdims=True)
            out = jnp.where(row == n, jnp.where(cnt < float(MOBA_TOPK), 1.0, 0.0), out)
        return jnp.where(past, out, 0.0)

    sel_t = jnp.concatenate([select(s_t[0:8]), select(s_t[8:16]), jnp.zeros((LANES - 16, blk), F32)], axis=0)
    sel_ref[...] = sel_t.T

    scale = HEAD_DIM ** -0.5
    zero = jnp.zeros_like(q2)
    q_heads = (jnp.where(first_head, q2, zero) * scale, jnp.where(first_head, zero, q2) * scale)

    def attend(kblk, vblk, masks, init):
        pvs, alphas = [], []
        for hd in range(2):
            s = lax.dot_general(q_heads[hd], kblk, _NT, preferred_element_type=F32)
            s = jnp.where(masks[hd], s, NEG_BIG)
            m_prev = jnp.full((blk, LANES), NEG_BIG, F32) if init else m_ref[hd]
            m_new = jnp.maximum(m_prev, jnp.max(s, axis=1, keepdims=True))
            alpha = jnp.exp(m_prev - m_new)
            p = jnp.exp(s - jnp.concatenate([m_new, m_new], axis=1))
            psum = jnp.sum(p, axis=1, keepdims=True)
            l_ref[hd] = psum + (jnp.zeros((blk, LANES), F32) if init else alpha * l_ref[hd])
            m_ref[hd] = m_new
            pvs.append(_dot(p.astype(BF16), vblk))
            alphas.append(alpha)
        pv = jnp.where(first_head, pvs[0], pvs[1])
        if init:
            acc_ref[...] = pv
        else:
            acc_ref[...] = acc_ref[...] * jnp.where(first_head, alphas[0], alphas[1]) + pv

    own = pl.multiple_of(qi * blk, blk)
    tri = lax.broadcasted_iota(jnp.int32, (blk, blk), 1) <= lax.broadcasted_iota(jnp.int32, (blk, blk), 0)
    attend(k_ref[0, pl.ds(own, blk), :], v_ref[0, pl.ds(own, blk), :], (tri, tri), True)

    for n in range(nb - 1):
        @pl.when(n < qi)
        def _(n=n):
            sel = sel_ref[...]
            masks = (sel[:, n:n + 1] > 0.5, sel[:, 8 + n:9 + n] > 0.5)
            attend(k_ref[0, n * blk:(n + 1) * blk, :], v_ref[0, n * blk:(n + 1) * blk, :], masks, False)

    inv = jnp.where(first_head, 1.0 / l_ref[0], 1.0 / l_ref[1])
    o_ref[0] = (acc_ref[...] * inv).astype(o_ref.dtype)


def _moba(qkv):
    b, s, w3 = qkv.shape
    blk = MOBA_BLOCK
    npair = ATT_HEADS // 2
    assert w3 == 3 * ATT_HEADS * HEAD_DIM and s % blk == 0 and s // blk <= 8
    nb = s // blk
    est = 2 * (2 * blk * LANES * 2 + 2 * s * LANES * 2) + 6 * blk * LANES * 4 + 8 * blk * blk * 4
    return pl.pallas_call(
        functools.partial(_moba_kernel, nb=nb),
        grid=(b, npair, nb),
        in_specs=[pl.BlockSpec((1, blk, LANES), lambda i, p, j: (i, j, p)),
                  pl.BlockSpec((1, s, LANES), lambda i, p, j: (i, 0, npair + p)),
                  pl.BlockSpec((1, s, LANES), lambda i, p, j: (i, 0, 2 * npair + p))],
        out_specs=pl.BlockSpec((1, blk, LANES), lambda i, p, j: (i, j, p)),
        out_shape=jax.ShapeDtypeStruct((b, s, ATT_HEADS * HEAD_DIM), BF16),
        scratch_shapes=[pltpu.VMEM((LANES, LANES), BF16), pltpu.VMEM((LANES, LANES), BF16),
                        pltpu.VMEM((2, blk, LANES), F32), pltpu.VMEM((2, blk, LANES), F32),
                        pltpu.VMEM((blk, LANES), F32), pltpu.VMEM((blk, LANES), F32)],
        compiler_params=pltpu.CompilerParams(dimension_semantics=("arbitrary", "arbitrary", "arbitrary"),
                                             vmem_limit_bytes=_vmem_limit(est)),
        name="moba",
    )(qkv, qkv, qkv)


def _gdn_kernel(x_ref, sm_ref, cw_ref, alog_ref, dtb_ref, ltri_ref, exp_ref, hsum_ref, o_ref, xs_ref, st_ref, *, rows):
    ck = DN_CHUNK
    dn_w = DN_HEADS * HEAD_DIM
    npair = DN_HEADS // 2
    c = pl.program_id(1)
    halo = 8

    @pl.when(c == 0)
    def _():
        xs_ref[0:halo, :] = jnp.zeros((halo, 3 * dn_w), F32)
        st_ref[...] = jnp.zeros_like(st_ref)

    xs_ref[halo:halo + rows, :] = x_ref[0].astype(F32)
    cw = cw_ref[...]
    y = xs_ref[halo - 3:halo - 3 + rows, :] * cw[0:1, :]
    for j in range(1, CONV_WIDTH):
        y = y + xs_ref[halo - 3 + j:halo - 3 + j + rows, :] * cw[j:j + 1, :]
    xs_ref[0:halo, :] = xs_ref[rows:rows + halo, :]
    y = y * _sigmoid(y)
    q, k, v = y[:, :dn_w], y[:, dn_w:2 * dn_w], y[:, 2 * dn_w:]

    hsum = hsum_ref[...]

    def head_sumsq(a):
        sq = a * a
        hi = sq.astype(BF16)
        lo = (sq - hi.astype(F32)).astype(BF16)
        return _dot(hi, hsum) + _dot(lo, hsum)

    qn = q * (lax.rsqrt(head_sumsq(q) + NORM_EPS) * (HEAD_DIM ** -0.5))
    kn = k * lax.rsqrt(head_sumsq(k) + NORM_EPS)

    sm = sm_ref[0]
    lane = lax.broadcasted_iota(jnp.int32, (rows, LANES), 1)
    xa = sm + dtb_ref[...]
    softplus = jnp.maximum(xa, 0.0) + jnp.log(1.0 + jnp.exp(-jnp.abs(xa)))
    g = -jnp.exp(alog_ref[...]) * softplus
    ltri = ltri_ref[...]
    gc = sum(_dot(ltri, t) for t in _split3(g))
    comb = jnp.where(lane < DN_HEADS, _sigmoid(sm), gc)
    expd = exp_ref[...]
    ex = sum(_dot(t, expd) for t in _split3(comb))
    bexp, gexp = ex[:, :dn_w], ex[:, dn_w:]

    eg = jnp.exp(gexp)
    vb = v * bexp
    kb = kn * bexp
    kbe = kb * eg
    qd = qn * eg

    ri = lax.broadcasted_iota(jnp.int32, (ck, dn_w), 0)
    ci = jnp.bitwise_and(lax.broadcasted_iota(jnp.int32, (ck, dn_w), 1), ck - 1)
    tri, strict, diag = ri >= ci, ri > ci, ri == ci
    r2 = lax.broadcasted_iota(jnp.int32, (LANES, LANES), 0) // HEAD_DIM
    c2 = lax.broadcasted_iota(jnp.int32, (LANES, LANES), 1) // HEAD_DIM
    bmask = r2 == c2
    eye2 = jnp.where(diag[:, :LANES], 1.0, 0.0)

    def bd2(a):
        return jnp.where(bmask, jnp.concatenate([a, a], axis=0), 0.0).astype(BF16)

    nchunk = rows // ck
    chains = [(cc, p) for cc in range(nchunk) for p in range(npair)]
    rs = lambda cc: slice(cc * ck, (cc + 1) * ck)
    ls = lambda p: slice(p * LANES, (p + 1) * LANES)

    gch = [gexp[rs(cc)] for cc in range(nchunk)]
    glast = [g_[ck - 1:ck, :] for g_ in gch]
    dec = []
    for g_ in gch:
        grow = jnp.sum(jnp.where(diag, g_, 0.0), axis=0, keepdims=True)
        dec.append(jnp.where(tri, jnp.exp(jnp.where(tri, g_ - grow, 0.0)), 0.0))

    kq = {(cc, p): lax.dot_general(jnp.concatenate([kb[rs(cc), ls(p)], qn[rs(cc), ls(p)]], axis=0).astype(BF16),
                                   bd2(kn[rs(cc), ls(p)]), _NT, preferred_element_type=F32)
          for cc, p in chains}
    neg_l = {(cc, p): -jnp.where(strict[:, ls(p)], kq[cc, p][:ck] * dec[cc][:, ls(p)], 0.0) for cc, p in chains}
    qk = {(cc, p): jnp.where(tri[:, ls(p)], kq[cc, p][ck:] * dec[cc][:, ls(p)], 0.0) for cc, p in chains}
    ssum = {ch: eye2 + neg_l[ch] for ch in chains}
    pw = {ch: _dot(neg_l[ch].astype(BF16), bd2(neg_l[ch])) for ch in chains}
    span = 2
    while span * 2 < ck:
        both = {ch: _dot(jnp.concatenate([ssum[ch], pw[ch]], axis=0).astype(BF16), bd2(pw[ch])) for ch in chains}
        ssum = {ch: ssum[ch] + both[ch][:ck] for ch in chains}
        pw = {ch: both[ch][ck:] for ch in chains}
        span *= 2
    corr = {ch: _dot(ssum[ch].astype(BF16), bd2(pw[ch])) for ch in chains}
    uw = {(cc, p): _dot((ssum[cc, p] + corr[cc, p]).astype(BF16),
                        jnp.concatenate([bd2(vb[rs(cc), ls(p)]), bd2(kbe[rs(cc), ls(p)])], axis=1))
          for cc, p in chains}

    state = [st_ref[p] for p in range(npair)]
    for cc in range(nchunk):
        kdec = kn[rs(cc)] * jnp.exp(glast[cc] - gch[cc])
        gl_exp = jnp.exp(glast[cc])
        wq = [_dot(jnp.concatenate([uw[cc, p][:, LANES:], qd[rs(cc), ls(p)]], axis=0).astype(BF16), state[p].astype(BF16))
              for p in range(npair)]
        vnew = [uw[cc, p][:, :LANES] - wq[p][:ck] for p in range(npair)]
        intra = [_dot(qk[cc, p].astype(BF16), bd2(vnew[p])) for p in range(npair)]
        upd = [lax.dot_general(kdec[:, ls(p)].astype(BF16), vnew[p].astype(BF16), _TN, preferred_element_type=F32)
               for p in range(npair)]
        for p in range(npair):
            o_ref[0, rs(cc), ls(p)] = (wq[p][ck:] + intra[p]).astype(o_ref.dtype)
        state = [state[p] * gl_exp[:, ls(p)] + jnp.where(bmask, upd[p], 0.0) for p in range(npair)]
    for p in range(npair):
        st_ref[p] = state[p]


def _gdn(qkv_dn, small, conv_w, a_log, dt_bias, *, rows):
    b, s, w3 = qkv_dn.shape
    dn_w = DN_HEADS * HEAD_DIM
    assert w3 == 3 * dn_w and s % rows == 0 and rows % DN_CHUNK == 0
    h = DN_HEADS
    alog_row = jnp.zeros((1, LANES), F32).at[0, h:2 * h].set(a_log.astype(F32))
    dtb_row = jnp.zeros((1, LANES), F32).at[0, h:2 * h].set(dt_bias.astype(F32))
    r = jnp.arange(rows)
    ltri = ((r[:, None] // DN_CHUNK == r[None, :] // DN_CHUNK) & (r[None, :] <= r[:, None])).astype(BF16)
    src = jnp.arange(LANES)[:, None]
    dst = jnp.arange(2 * dn_w)[None, :]
    expander = ((src < 2 * h) & (dst // dn_w == src // h) & ((dst % dn_w) // HEAD_DIM == src % h)).astype(BF16)
    hl = jnp.arange(dn_w) // HEAD_DIM
    hsum = (hl[:, None] == hl[None, :]).astype(BF16)
    const = lambda a: pl.BlockSpec(a.shape, lambda i, j: (0,) * a.ndim)
    est = (2 * (rows * w3 * 2 + rows * LANES * 4 + rows * dn_w * 4) + (rows + 8) * w3 * 4 + 12 * rows * w3 * 4
           + 2 * (ltri.size + expander.size + hsum.size) * 2)
    return pl.pallas_call(
        functools.partial(_gdn_kernel, rows=rows),
        grid=(b, s // rows),
        in_specs=[pl.BlockSpec((1, rows, w3), lambda i, j: (i, j, 0)),
                  pl.BlockSpec((1, rows, LANES), lambda i, j: (i, j, 0)),
                  const(conv_w), const(alog_row), const(dtb_row), const(ltri), const(expander), const(hsum)],
        out_specs=pl.BlockSpec((1, rows, dn_w), lambda i, j: (i, j, 0)),
        out_shape=jax.ShapeDtypeStruct((b, s, dn_w), F32),
        scratch_shapes=[pltpu.VMEM((rows + 8, w3), F32), pltpu.VMEM((DN_HEADS // 2, LANES, LANES), F32)],
        compiler_params=pltpu.CompilerParams(dimension_semantics=("arbitrary", "arbitrary"),
                                             vmem_limit_bytes=_vmem_limit(est)),
        name="gdn",
    )(qkv_dn, small, conv_w.astype(F32), alog_row, dtb_row, ltri, expander, hsum)


def _mixout_kernel(x_ref, ya_ref, od_ref, z_ref, gate_ref, dnw_ref, hmean_ref, wa_ref, wd_ref, wo_ref, pn_ref, o_ref):
    d = x_ref.shape[1]
    od = od_ref[...]
    sq = od * od
    hi = sq.astype(BF16)
    lo = (sq - hi.astype(F32)).astype(BF16)
    ms = _dot(hi, hmean_ref[...]) + _dot(lo, hmean_ref[...])
    z = z_ref[...].astype(F32)
    y_dn = od * lax.rsqrt(ms + NORM_EPS) * dnw_ref[...] * (z * _sigmoid(z))
    ga = _sigmoid(gate_ref[:, :d].astype(F32))
    gd = _sigmoid(gate_ref[:, d:].astype(F32))
    merged = ga * _dot(ya_ref[...], wa_ref[...]) + gd * _dot(y_dn.astype(BF16), wd_ref[...])
    y = _dot(merged.astype(BF16), wo_ref[...])
    o_ref[...] = x_ref[...] + _rms(y, pn_ref[...])


def _mixout(x2, y_att, o_dn, z, gates, dn_norm, wa, wd, wo, post_norm, *, tm):
    t, d = x2.shape
    dn_w = DN_HEADS * HEAD_DIM
    dnw_row = jnp.tile(dn_norm.astype(F32), DN_HEADS)[None, :]
    hl = jnp.arange(dn_w) // HEAD_DIM
    hmean = ((hl[:, None] == hl[None, :]).astype(F32) / HEAD_DIM).astype(BF16)
    row = lambda w: pl.BlockSpec((tm, w), lambda i: (i, 0))
    const = lambda a: pl.BlockSpec(a.shape, lambda i: (0, 0))
    est = (2 * (2 * tm * d * 4 + tm * dn_w * (2 + 4 + 2) + tm * 2 * d * 2)
           + 2 * (hmean.size + wa.size + wd.size + wo.size) * 2 + 8 * tm * d * 4)
    return pl.pallas_call(
        _mixout_kernel,
        grid=(t // tm,),
        in_specs=[row(d), row(dn_w), row(dn_w), row(dn_w), row(2 * d), const(dnw_row), const(hmean),
                  const(wa), const(wd), const(wo), const(post_norm)],
        out_specs=row(d),
        out_shape=jax.ShapeDtypeStruct((t, d), F32),
        compiler_params=pltpu.CompilerParams(dimension_semantics=("arbitrary",), vmem_limit_bytes=_vmem_limit(est)),
        name="mixout",
    )(x2, y_att, o_dn, z, gates, dnw_row, hmean, wa, wd, wo, post_norm)


def _mlp_kernel(x_ref, pre_ref, w1_ref, w2_ref, post_ref, o_ref, *, ff_chunk):
    x = x_ref[...]
    hb = _rms(x, pre_ref[...]).astype(BF16)
    acc = jnp.zeros(x.shape, F32)
    for c in range(0, w1_ref.shape[1], ff_chunk):
        a = jnp.maximum(_dot(hb, w1_ref[:, c:c + ff_chunk]), 0.0)
        acc = acc + _dot((a * a).astype(BF16), w2_ref[c:c + ff_chunk, :])
    o_ref[...] = x + _rms(acc, post_ref[...])


def _mlp(x1, pre, w1, w2, post, *, tm):
    t, d = x1.shape
    row = pl.BlockSpec((tm, d), lambda i: (i, 0))
    const = lambda a: pl.BlockSpec(a.shape, lambda i: (0, 0))
    est = 2 * (2 * tm * d * 4 + (w1.size + w2.size) * 2) + 6 * tm * d * 4 + 2 * tm * 1024 * 4
    return pl.pallas_call(
        functools.partial(_mlp_kernel, ff_chunk=1024),
        grid=(t // tm,),
        in_specs=[row, const(pre), const(w1), const(w2), const(post)],
        out_specs=row,
        out_shape=jax.ShapeDtypeStruct((t, d), F32),
        compiler_params=pltpu.CompilerParams(dimension_semantics=("arbitrary",), vmem_limit_bytes=_vmem_limit(est)),
        name="mlp",
    )(x1, pre, w1, w2, post)


def kernel(x, pre_norm_mix, w_in, conv_w, a_log, dt_bias, dn_norm, w_branch_att, w_branch_dn,
           w_out, post_norm_mix, pre_norm_mlp, w_mlp_in, w_mlp_out, post_norm_mlp):
    b, s, d = x.shape
    att_w = ATT_HEADS * HEAD_DIM
    dn_w = DN_HEADS * HEAD_DIM
    n_main = 3 * att_w + 3 * dn_w + dn_w
    tm = 512
    x2 = x.reshape(b * s, d)
    for l in range(w_in.shape[0]):
        wl = w_in[l]
        w_main = jnp.concatenate([wl[:, :n_main], wl[:, n_main + 2 * DN_HEADS:]], axis=1).astype(BF16)
        w_small = jnp.pad(wl[:, n_main:n_main + 2 * DN_HEADS], ((0, 0), (0, LANES - 2 * DN_HEADS))).astype(BF16)
        qkv_att, qkv_dn, z, gates, small = _inproj(x2, pre_norm_mix[l][None, :], w_main, w_small, tm=tm)
        y_att = _moba(qkv_att.reshape(b, s, 3 * att_w))
        o_dn = _gdn(qkv_dn.reshape(b, s, 3 * dn_w), small.reshape(b, s, LANES), conv_w[l], a_log[l], dt_bias[l], rows=256)
        x2 = _mixout(x2, y_att.reshape(b * s, att_w), o_dn.reshape(b * s, dn_w), z, gates, dn_norm[l],
                     w_branch_att[l].astype(BF16), w_branch_dn[l].astype(BF16), w_out[l].astype(BF16),
                     post_norm_mix[l][None, :], tm=tm)
        x2 = _mlp(x2, pre_norm_mlp[l][None, :], w_mlp_in[l].astype(BF16), w_mlp_out[l].astype(BF16),
                  post_norm_mlp[l][None, :], tm=tm)
    return x2.reshape(b, s, d)
```

```python
import functools

import jax
import jax.numpy as jnp
from jax import lax
from jax.experimental import pallas as pl
from jax.experimental.pallas import tpu as pltpu

ATT_HEADS = 8
DN_HEADS = 8
HEAD_DIM = 64
MOBA_BLOCK = 256
MOBA_TOPK = 3
DN_CHUNK = 64
CONV_WIDTH = 4
NORM_EPS = 1e-6

LANES = 128
V7X_VMEM_BYTES = 64 * 1024 * 1024
VMEM_CAP_BYTES = 56 * 1024 * 1024

F32 = jnp.float32
BF16 = jnp.bfloat16
NEG_BIG = -1e30
MASK_BIAS = -(2.0 ** 100)

_NT = (((1,), (1,)), ((), ()))
_TN = (((0,), (0,)), ((), ()))


def _vmem_limit(nbytes):
    return int(min(VMEM_CAP_BYTES, nbytes * 5 // 4 + (4 << 20)))


def _dot(a, b):
    return jnp.dot(a, b, preferred_element_type=F32)


def _rms(x, w):
    return x * lax.rsqrt(jnp.mean(x * x, axis=-1, keepdims=True) + NORM_EPS) * w


def _sigmoid(x):
    return 1.0 / (1.0 + jnp.exp(-x))


def _split3(x):
    hi = x.astype(BF16)
    r = x - hi.astype(F32)
    mid = r.astype(BF16)
    lo = (r - mid.astype(F32)).astype(BF16)
    return hi, mid, lo


def _inproj_kernel(x_ref, g_ref, wm_ref, ws_ref, att_ref, dn_ref, z_ref, gate_ref, small_ref, *, col_chunk):
    hb = _rms(x_ref[...], g_ref[...]).astype(BF16)
    start = 0
    for ref in (att_ref, dn_ref, z_ref, gate_ref):
        width = ref.shape[1]
        for c in range(0, width, col_chunk):
            ref[:, c:c + col_chunk] = _dot(hb, wm_ref[:, start + c:start + c + col_chunk]).astype(ref.dtype)
        start += width
    small_ref[...] = _dot(hb, ws_ref[...])


def _inproj(x2, gain, w_main, w_small, *, tm):
    t, d = x2.shape
    att_w, dn_w, z_w, gate_w = 3 * ATT_HEADS * HEAD_DIM, 3 * DN_HEADS * HEAD_DIM, DN_HEADS * HEAD_DIM, 2 * d
    assert w_main.shape == (d, att_w + dn_w + z_w + gate_w) and t % tm == 0
    row = lambda w: pl.BlockSpec((tm, w), lambda i: (i, 0))
    const = lambda shp: pl.BlockSpec(shp, lambda i: (0, 0))
    est = 2 * (tm * d * 4 + w_main.size * 2 + w_small.size * 2 + tm * w_main.shape[1] * 2 + tm * LANES * 4) + 4 * tm * d * 4
    return pl.pallas_call(
        functools.partial(_inproj_kernel, col_chunk=512),
        grid=(t // tm,),
        in_specs=[row(d), const((1, d)), const(w_main.shape), const(w_small.shape)],
        out_specs=[row(att_w), row(dn_w), row(z_w), row(gate_w), row(LANES)],
        out_shape=[jax.ShapeDtypeStruct((t, att_w), BF16), jax.ShapeDtypeStruct((t, dn_w), BF16),
                   jax.ShapeDtypeStruct((t, z_w), BF16), jax.ShapeDtypeStruct((t, gate_w), BF16),
                   jax.ShapeDtypeStruct((t, LANES), F32)],
        compiler_params=pltpu.CompilerParams(dimension_semantics=("arbitrary",), vmem_limit_bytes=_vmem_limit(est)),
        name="inproj",
    )(x2, gain, w_main, w_small)


def _moba_kernel(q_ref, k_ref, v_ref, o_ref, *, nb):
    blk = MOBA_BLOCK
    lane = lax.broadcasted_iota(jnp.int32, (blk, LANES), 1)
    first_head = lane < HEAD_DIM
    scale = HEAD_DIM ** -0.5

    km = jnp.concatenate(
        [jnp.sum(k_ref[0, n * blk:(n + 1) * blk, :].astype(F32), axis=0, keepdims=True) for n in range(nb)]
        + [jnp.zeros((8 - nb, LANES), F32)] * (1 if nb < 8 else 0), axis=0) * (1.0 / blk)
    l8 = lax.broadcasted_iota(jnp.int32, (8, LANES), 1) < HEAD_DIM
    pad = jnp.zeros((HEAD_DIM - 8, LANES), F32)
    g = jnp.concatenate([jnp.where(l8, 0.0, km), pad, jnp.where(l8, km, 0.0), pad], axis=0)
    g_hi = g.astype(BF16)
    g_lo = (g - g_hi.astype(F32)).astype(BF16)

    s_len = nb * blk
    kblk = lax.broadcasted_iota(jnp.int32, (s_len, LANES), 0) // blk
    klane = lax.broadcasted_iota(jnp.int32, (s_len, LANES), 1)
    k2 = k_ref[0]
    k_aug = (jnp.where(klane < HEAD_DIM, k2, jnp.where(klane - HEAD_DIM == kblk, 1.0, 0.0).astype(BF16)),
             jnp.where(klane < HEAD_DIM, jnp.where(klane == kblk, 1.0, 0.0).astype(BF16), k2))

    row = lax.broadcasted_iota(jnp.int32, (8, blk), 0)

    def drop_bias(s, j):
        out = jnp.zeros((8, blk), F32)
        past = row < j
        for n in range(j):
            rn = s[n:n + 1, :]
            beats = jnp.where(s > rn, 1.0, jnp.where(jnp.logical_and(s == rn, row < n), 1.0, 0.0))
            cnt = jnp.sum(jnp.where(past, beats, 0.0), axis=0, keepdims=True)
            out = jnp.where(row == n, jnp.where(cnt < float(MOBA_TOPK), 0.0, MASK_BIAS), out)
        return out

    def queries(j):
        q2 = q_ref[0, j * blk:(j + 1) * blk, :]
        qs = q2 * scale
        zero = jnp.zeros_like(qs)
        if j <= MOBA_TOPK:
            return jnp.where(first_head, qs, zero), jnp.where(first_head, zero, qs)
        s_t = (lax.dot_general(g_hi, q2, _NT, preferred_element_type=F32)
               + lax.dot_general(g_lo, q2, _NT, preferred_element_type=F32))
        padq = jnp.zeros((HEAD_DIM - 8, blk), F32)
        bias_t = jnp.concatenate([drop_bias(s_t[0:8], j), padq, drop_bias(s_t[HEAD_DIM:HEAD_DIM + 8], j), padq], axis=0)
        bias = bias_t.T.astype(BF16)
        return jnp.where(first_head, qs, bias), jnp.where(first_head, bias, qs)

    def scores(j):
        qa, qb = queries(j)
        width = (j + 1) * blk
        return (lax.dot_general(qa, k_aug[0][0:width], _NT, preferred_element_type=F32),
                lax.dot_general(qb, k_aug[1][0:width], _NT, preferred_element_type=F32))

    tri = lax.broadcasted_iota(jnp.int32, (blk, blk), 1) <= lax.broadcasted_iota(jnp.int32, (blk, blk), 0)

    sc = scores(0)
    for j in range(nb):
        nxt = scores(j + 1) if j + 1 < nb else None
        width = (j + 1) * blk
        vj = v_ref[0, 0:width, :]
        outs = []
        for hd in range(2):
            parts = [sc[hd][:, n * blk:(n + 1) * blk] for n in range(j)]
            parts.append(jnp.where(tri, sc[hd][:, j * blk:width], NEG_BIG))
            mx = parts[0]
            for pt in parts[1:]:
                mx = jnp.maximum(mx, pt)
            m = jnp.max(mx, axis=1, keepdims=True)
            ps = [jnp.exp(pt - m) for pt in parts]
            tot = ps[0]
            for pp in ps[1:]:
                tot = tot + pp
            inv = 1.0 / jnp.sum(tot, axis=1, keepdims=True)
            outs.append(_dot(jnp.concatenate([pp.astype(BF16) for pp in ps], axis=1), vj) * inv)
        o_ref[0, j * blk:(j + 1) * blk, :] = jnp.where(first_head, outs[0], outs[1]).astype(o_ref.dtype)
        sc = nxt


def _moba(qkv):
    b, s, w3 = qkv.shape
    blk = MOBA_BLOCK
    npair = ATT_HEADS // 2
    assert w3 == 3 * ATT_HEADS * HEAD_DIM and s % blk == 0 and s // blk <= 8
    nb = s // blk
    est = 2 * 4 * s * LANES * 2 + 4 * blk * s * (4 + 4 + 2)
    return pl.pallas_call(
        functools.partial(_moba_kernel, nb=nb),
        grid=(b, npair),
        in_specs=[pl.BlockSpec((1, s, LANES), lambda i, p: (i, 0, p)),
                  pl.BlockSpec((1, s, LANES), lambda i, p: (i, 0, npair + p)),
                  pl.BlockSpec((1, s, LANES), lambda i, p: (i, 0, 2 * npair + p))],
        out_specs=pl.BlockSpec((1, s, LANES), lambda i, p: (i, 0, p)),
        out_shape=jax.ShapeDtypeStruct((b, s, ATT_HEADS * HEAD_DIM), BF16),
        compiler_params=pltpu.CompilerParams(dimension_semantics=("arbitrary", "arbitrary"),
                                             vmem_limit_bytes=_vmem_limit(est)),
        name="moba",
    )(qkv, qkv, qkv)


def _gdn_kernel(x_ref, sm_ref, cw_ref, alog_ref, dtb_ref, ltri_ref, exp_ref, hsum_ref, o_ref, xs_ref, st_ref, *, rows):
    ck = DN_CHUNK
    dn_w = DN_HEADS * HEAD_DIM
    npair = DN_HEADS // 2
    c = pl.program_id(1)
    halo = 8

    @pl.when(c == 0)
    def _():
        xs_ref[0:halo, :] = jnp.zeros((halo, 3 * dn_w), F32)
        st_ref[...] = jnp.zeros_like(st_ref)

    xs_ref[halo:halo + rows, :] = x_ref[0].astype(F32)
    cw = cw_ref[...]
    y = xs_ref[halo - 3:halo - 3 + rows, :] * cw[0:1, :]
    for j in range(1, CONV_WIDTH):
        y = y + xs_ref[halo - 3 + j:halo - 3 + j + rows, :] * cw[j:j + 1, :]
    xs_ref[0:halo, :] = xs_ref[rows:rows + halo, :]
    y = y * _sigmoid(y)
    q, k, v = y[:, :dn_w], y[:, dn_w:2 * dn_w], y[:, 2 * dn_w:]

    hsum = hsum_ref[...]

    def head_sumsq(a):
        return _dot((a * a).astype(BF16), hsum)

    qn = q * (lax.rsqrt(head_sumsq(q) + NORM_EPS) * (HEAD_DIM ** -0.5))
    kn = k * lax.rsqrt(head_sumsq(k) + NORM_EPS)

    sm = sm_ref[0]
    lane = lax.broadcasted_iota(jnp.int32, (rows, LANES), 1)
    xa = sm + dtb_ref[...]
    softplus = jnp.maximum(xa, 0.0) + jnp.log(1.0 + jnp.exp(-jnp.abs(xa)))
    g = -jnp.exp(alog_ref[...]) * softplus
    ltri = ltri_ref[...]
    gc = sum(_dot(ltri, t) for t in _split3(g))
    comb = jnp.where(lane < DN_HEADS, _sigmoid(sm), gc)
    expd = exp_ref[...]
    ex = sum(_dot(t, expd) for t in _split3(comb))
    bexp, gexp = ex[:, :dn_w], ex[:, dn_w:]

    eg = jnp.exp(gexp)
    vb = v * bexp
    kb = kn * bexp
    kbe = kb * eg
    qd = qn * eg

    ri = lax.broadcasted_iota(jnp.int32, (ck, dn_w), 0)
    ci = jnp.bitwise_and(lax.broadcasted_iota(jnp.int32, (ck, dn_w), 1), ck - 1)
    tri, strict, diag = ri >= ci, ri > ci, ri == ci
    r2 = lax.broadcasted_iota(jnp.int32, (LANES, LANES), 0) // HEAD_DIM
    c2 = lax.broadcasted_iota(jnp.int32, (LANES, LANES), 1) // HEAD_DIM
    bmask = r2 == c2
    eye2 = jnp.where(diag[:, :LANES], 1.0, 0.0)

    def bd2(a):
        return jnp.where(bmask, jnp.concatenate([a, a], axis=0), 0.0).astype(BF16)

    nchunk = rows // ck
    chains = [(cc, p) for cc in range(nchunk) for p in range(npair)]
    rs = lambda cc: slice(cc * ck, (cc + 1) * ck)
    ls = lambda p: slice(p * LANES, (p + 1) * LANES)

    gch = [gexp[rs(cc)] for cc in range(nchunk)]
    glast = [g_[ck - 1:ck, :] for g_ in gch]
    dec = []
    for g_ in gch:
        grow = jnp.sum(jnp.where(diag, g_, 0.0), axis=0, keepdims=True)
        dec.append(jnp.where(tri, jnp.exp(jnp.where(tri, g_ - grow, 0.0)), 0.0))

    kq = {(cc, p): lax.dot_general(jnp.concatenate([kb[rs(cc), ls(p)], qn[rs(cc), ls(p)]], axis=0).astype(BF16),
                                   bd2(kn[rs(cc), ls(p)]), _NT, preferred_element_type=F32)
          for cc, p in chains}
    neg_l = {(cc, p): -jnp.where(strict[:, ls(p)], kq[cc, p][:ck] * dec[cc][:, ls(p)], 0.0) for cc, p in chains}
    qk = {(cc, p): jnp.where(tri[:, ls(p)], kq[cc, p][ck:] * dec[cc][:, ls(p)], 0.0) for cc, p in chains}
    ssum = {ch: eye2 + neg_l[ch] for ch in chains}
    pw = {ch: _dot(neg_l[ch].astype(BF16), bd2(neg_l[ch])) for ch in chains}
    span = 2
    while span * 2 < ck:
        both = {ch: _dot(jnp.concatenate([ssum[ch], pw[ch]], axis=0).astype(BF16), bd2(pw[ch])) for ch in chains}
        ssum = {ch: ssum[ch] + both[ch][:ck] for ch in chains}
        pw = {ch: both[ch][ck:] for ch in chains}
        span *= 2
    corr = {ch: _dot(ssum[ch].astype(BF16), bd2(pw[ch])) for ch in chains}
    uw = {(cc, p): _dot((ssum[cc, p] + corr[cc, p]).astype(BF16),
                        jnp.concatenate([bd2(vb[rs(cc), ls(p)]), bd2(kbe[rs(cc), ls(p)])], axis=1))
          for cc, p in chains}

    state = [st_ref[p] for p in range(npair)]
    for cc in range(nchunk):
        kdec = kn[rs(cc)] * jnp.exp(glast[cc] - gch[cc])
        gl_exp = jnp.exp(glast[cc])
        wq = [_dot(jnp.concatenate([uw[cc, p][:, LANES:], qd[rs(cc), ls(p)]], axis=0).astype(BF16), state[p].astype(BF16))
              for p in range(npair)]
        vnew = [uw[cc, p][:, :LANES] - wq[p][:ck] for p in range(npair)]
        intra = [_dot(qk[cc, p].astype(BF16), bd2(vnew[p])) for p in range(npair)]
        upd = [lax.dot_general(kdec[:, ls(p)].astype(BF16), vnew[p].astype(BF16), _TN, preferred_element_type=F32)
               for p in range(npair)]
        for p in range(npair):
            o_ref[0, rs(cc), ls(p)] = (wq[p][ck:] + intra[p]).astype(o_ref.dtype)
        state = [state[p] * gl_exp[:, ls(p)] + jnp.where(bmask, upd[p], 0.0) for p in range(npair)]
    for p in range(npair):
        st_ref[p] = state[p]


def _gdn(qkv_dn, small, conv_w, a_log, dt_bias, *, rows):
    b, s, w3 = qkv_dn.shape
    dn_w = DN_HEADS * HEAD_DIM
    assert w3 == 3 * dn_w and s % rows == 0 and rows % DN_CHUNK == 0
    h = DN_HEADS
    alog_row = jnp.zeros((1, LANES), F32).at[0, h:2 * h].set(a_log.astype(F32))
    dtb_row = jnp.zeros((1, LANES), F32).at[0, h:2 * h].set(dt_bias.astype(F32))
    r = jnp.arange(rows)
    ltri = ((r[:, None] // DN_CHUNK == r[None, :] // DN_CHUNK) & (r[None, :] <= r[:, None])).astype(BF16)
    src = jnp.arange(LANES)[:, None]
    dst = jnp.arange(2 * dn_w)[None, :]
    expander = ((src < 2 * h) & (dst // dn_w == src // h) & ((dst % dn_w) // HEAD_DIM == src % h)).astype(BF16)
    hl = jnp.arange(dn_w) // HEAD_DIM
    hsum = (hl[:, None] == hl[None, :]).astype(BF16)
    const = lambda a: pl.BlockSpec(a.shape, lambda i, j: (0,) * a.ndim)
    est = (2 * (rows * w3 * 2 + rows * LANES * 4 + rows * dn_w * 4) + (rows + 8) * w3 * 4 + 12 * rows * w3 * 4
           + 2 * (ltri.size + expander.size + hsum.size) * 2)
    return pl.pallas_call(
        functools.partial(_gdn_kernel, rows=rows),
        grid=(b, s // rows),
        in_specs=[pl.BlockSpec((1, rows, w3), lambda i, j: (i, j, 0)),
                  pl.BlockSpec((1, rows, LANES), lambda i, j: (i, j, 0)),
                  const(conv_w), const(alog_row), const(dtb_row), const(ltri), const(expander), const(hsum)],
        out_specs=pl.BlockSpec((1, rows, dn_w), lambda i, j: (i, j, 0)),
        out_shape=jax.ShapeDtypeStruct((b, s, dn_w), F32),
        scratch_shapes=[pltpu.VMEM((rows + 8, w3), F32), pltpu.VMEM((DN_HEADS // 2, LANES, LANES), F32)],
        compiler_params=pltpu.CompilerParams(dimension_semantics=("arbitrary", "arbitrary"),
                                             vmem_limit_bytes=_vmem_limit(est)),
        name="gdn",
    )(qkv_dn, small, conv_w.astype(F32), alog_row, dtb_row, ltri, expander, hsum)


def _mixout_kernel(x_ref, ya_ref, od_ref, z_ref, gate_ref, dnw_ref, hmean_ref, wa_ref, wd_ref, wo_ref, pn_ref, o_ref):
    d = x_ref.shape[1]
    od = od_ref[...]
    ms = _dot((od * od).astype(BF16), hmean_ref[...])
    z = z_ref[...].astype(F32)
    y_dn = od * lax.rsqrt(ms + NORM_EPS) * dnw_ref[...] * (z * _sigmoid(z))
    ga = _sigmoid(gate_ref[:, :d].astype(F32))
    gd = _sigmoid(gate_ref[:, d:].astype(F32))
    merged = ga * _dot(ya_ref[...], wa_ref[...]) + gd * _dot(y_dn.astype(BF16), wd_ref[...])
    y = _dot(merged.astype(BF16), wo_ref[...])
    o_ref[...] = x_ref[...] + _rms(y, pn_ref[...])


def _mixout(x2, y_att, o_dn, z, gates, dn_norm, wa, wd, wo, post_norm, *, tm):
    t, d = x2.shape
    dn_w = DN_HEADS * HEAD_DIM
    dnw_row = jnp.tile(dn_norm.astype(F32), DN_HEADS)[None, :]
    hl = jnp.arange(dn_w) // HEAD_DIM
    hmean = ((hl[:, None] == hl[None, :]).astype(F32) / HEAD_DIM).astype(BF16)
    row = lambda w: pl.BlockSpec((tm, w), lambda i: (i, 0))
    const = lambda a: pl.BlockSpec(a.shape, lambda i: (0, 0))
    est = (2 * (2 * tm * d * 4 + tm * dn_w * (2 + 4 + 2) + tm * 2 * d * 2)
           + 2 * (hmean.size + wa.size + wd.size + wo.size) * 2 + 8 * tm * d * 4)
    return pl.pallas_call(
        _mixout_kernel,
        grid=(t // tm,),
        in_specs=[row(d), row(dn_w), row(dn_w), row(dn_w), row(2 * d), const(dnw_row), const(hmean),
                  const(wa), const(wd), const(wo), const(post_norm)],
        out_specs=row(d),
        out_shape=jax.ShapeDtypeStruct((t, d), F32),
        compiler_params=pltpu.CompilerParams(dimension_semantics=("arbitrary",), vmem_limit_bytes=_vmem_limit(est)),
        name="mixout",
    )(x2, y_att, o_dn, z, gates, dnw_row, hmean, wa, wd, wo, post_norm)


def _mlp_kernel(x_ref, pre_ref, w1_ref, w2_ref, post_ref, o_ref, *, ff_chunk):
    x = x_ref[...]
    hb = _rms(x, pre_ref[...]).astype(BF16)
    acc = jnp.zeros(x.shape, F32)
    for c in range(0, w1_ref.shape[1], ff_chunk):
        a = jnp.maximum(_dot(hb, w1_ref[:, c:c + ff_chunk]), 0.0)
        acc = acc + _dot((a * a).astype(BF16), w2_ref[c:c + ff_chunk, :])
    o_ref[...] = x + _rms(acc, post_ref[...])


def _mlp(x1, pre, w1, w2, post, *, tm):
    t, d = x1.shape
    row = pl.BlockSpec((tm, d), lambda i: (i, 0))
    const = lambda a: pl.BlockSpec(a.shape, lambda i: (0, 0))
    est = 2 * (2 * tm * d * 4 + (w1.size + w2.size) * 2) + 6 * tm * d * 4 + 2 * tm * 1024 * 4
    return pl.pallas_call(
        functools.partial(_mlp_kernel, ff_chunk=1024),
        grid=(t // tm,),
        in_specs=[row, const(pre), const(w1), const(w2), const(post)],
        out_specs=row,
        out_shape=jax.ShapeDtypeStruct((t, d), F32),
        compiler_params=pltpu.CompilerParams(dimension_semantics=("arbitrary",), vmem_limit_bytes=_vmem_limit(est)),
        name="mlp",
    )(x1, pre, w1, w2, post)


def kernel(x, pre_norm_mix, w_in, conv_w, a_log, dt_bias, dn_norm, w_branch_att, w_branch_dn,
           w_out, post_norm_mix, pre_norm_mlp, w_mlp_in, w_mlp_out, post_norm_mlp):
    b, s, d = x.shape
    att_w = ATT_HEADS * HEAD_DIM
    dn_w = DN_HEADS * HEAD_DIM
    n_main = 3 * att_w + 3 * dn_w + dn_w
    tm = 512
    x2 = x.reshape(b * s, d)
    for l in range(w_in.shape[0]):
        wl = w_in[l]
        w_main = jnp.concatenate([wl[:, :n_main], wl[:, n_main + 2 * DN_HEADS:]], axis=1).astype(BF16)
        w_small = jnp.pad(wl[:, n_main:n_main + 2 * DN_HEADS], ((0, 0), (0, LANES - 2 * DN_HEADS))).astype(BF16)
        qkv_att, qkv_dn, z, gates, small = _inproj(x2, pre_norm_mix[l][None, :], w_main, w_small, tm=tm)
        y_att = _moba(qkv_att.reshape(b, s, 3 * att_w))
        o_dn = _gdn(qkv_dn.reshape(b, s, 3 * dn_w), small.reshape(b, s, LANES), conv_w[l], a_log[l], dt_bias[l], rows=256)
        x2 = _mixout(x2, y_att.reshape(b * s, att_w), o_dn.reshape(b * s, dn_w), z, gates, dn_norm[l],
                     w_branch_att[l].astype(BF16), w_branch_dn[l].astype(BF16), w_out[l].astype(BF16),
                     post_norm_mix[l][None, :], tm=tm)
        x2 = _mlp(x2, pre_norm_mlp[l][None, :], w_mlp_in[l].astype(BF16), w_mlp_out[l].astype(BF16),
                  post_norm_mlp[l][None, :], tm=tm)
    return x2.reshape(b, s, d)
```

```python
import functools

import jax
import jax.numpy as jnp
from jax import lax
from jax.experimental import pallas as pl
from jax.experimental.pallas import tpu as pltpu

ATT_HEADS = 8
DN_HEADS = 8
HEAD_DIM = 64
MOBA_BLOCK = 256
MOBA_TOPK = 3
DN_CHUNK = 64
CONV_WIDTH = 4
NORM_EPS = 1e-6

LANES = 128
V7X_VMEM_BYTES = 64 * 1024 * 1024
VMEM_CAP_BYTES = 56 * 1024 * 1024

F32 = jnp.float32
BF16 = jnp.bfloat16
NEG_BIG = -1e30
MASK_BIAS = -(2.0 ** 100)
LOG2_E = 1.4426950408889634

_NT = (((1,), (1,)), ((), ()))
_TN = (((0,), (0,)), ((), ()))


def _vmem_limit(nbytes):
    return int(min(VMEM_CAP_BYTES, nbytes * 5 // 4 + (4 << 20)))


def _dot(a, b):
    return jnp.dot(a, b, preferred_element_type=F32)


def _rms(x, w):
    return x * lax.rsqrt(jnp.mean(x * x, axis=-1, keepdims=True) + NORM_EPS) * w


def _sigmoid(x):
    return 1.0 / (1.0 + jnp.exp(-x))


def _split3(x):
    hi = x.astype(BF16)
    r = x - hi.astype(F32)
    mid = r.astype(BF16)
    lo = (r - mid.astype(F32)).astype(BF16)
    return hi, mid, lo


def _inproj_kernel(x_ref, g_ref, wm_ref, ws_ref, att_ref, dn_ref, z_ref, gate_ref, small_ref, *, col_chunk):
    hb = _rms(x_ref[...], g_ref[...]).astype(BF16)
    start = 0
    for ref in (att_ref, dn_ref, z_ref, gate_ref):
        width = ref.shape[1]
        for c in range(0, width, col_chunk):
            ref[:, c:c + col_chunk] = _dot(hb, wm_ref[:, start + c:start + c + col_chunk]).astype(ref.dtype)
        start += width
    small_ref[...] = _dot(hb, ws_ref[...])


def _inproj(x2, gain, w_main, w_small, *, tm):
    t, d = x2.shape
    att_w, dn_w, z_w, gate_w = 3 * ATT_HEADS * HEAD_DIM, 3 * DN_HEADS * HEAD_DIM, DN_HEADS * HEAD_DIM, 2 * d
    assert w_main.shape == (d, att_w + dn_w + z_w + gate_w) and t % tm == 0
    row = lambda w: pl.BlockSpec((tm, w), lambda i: (i, 0))
    const = lambda shp: pl.BlockSpec(shp, lambda i: (0, 0))
    est = 2 * (tm * d * 4 + w_main.size * 2 + w_small.size * 2 + tm * w_main.shape[1] * 2 + tm * LANES * 4) + 4 * tm * d * 4
    return pl.pallas_call(
        functools.partial(_inproj_kernel, col_chunk=512),
        grid=(t // tm,),
        in_specs=[row(d), const((1, d)), const(w_main.shape), const(w_small.shape)],
        out_specs=[row(att_w), row(dn_w), row(z_w), row(gate_w), row(LANES)],
        out_shape=[jax.ShapeDtypeStruct((t, att_w), BF16), jax.ShapeDtypeStruct((t, dn_w), BF16),
                   jax.ShapeDtypeStruct((t, z_w), BF16), jax.ShapeDtypeStruct((t, gate_w), BF16),
                   jax.ShapeDtypeStruct((t, LANES), F32)],
        compiler_params=pltpu.CompilerParams(dimension_semantics=("arbitrary",), vmem_limit_bytes=_vmem_limit(est)),
        name="inproj",
    )(x2, gain, w_main, w_small)


def _moba_kernel(q_ref, k_ref, v_ref, o_ref, *, nb):
    blk = MOBA_BLOCK
    lane = lax.broadcasted_iota(jnp.int32, (blk, LANES), 1)
    first_head = lane < HEAD_DIM

    km = jnp.concatenate(
        [jnp.sum(k_ref[0, n * blk:(n + 1) * blk, :].astype(F32), axis=0, keepdims=True) for n in range(nb)]
        + [jnp.zeros((8 - nb, LANES), F32)] * (1 if nb < 8 else 0), axis=0) * (1.0 / blk)
    l8 = lax.broadcasted_iota(jnp.int32, (8, LANES), 1) < HEAD_DIM
    pad = jnp.zeros((HEAD_DIM - 8, LANES), F32)
    g = jnp.concatenate([jnp.where(l8, 0.0, km), pad, jnp.where(l8, km, 0.0), pad], axis=0)
    g_hi = g.astype(BF16)
    g_lo = (g - g_hi.astype(F32)).astype(BF16)

    s_len = nb * blk
    kblk = lax.broadcasted_iota(jnp.int32, (s_len, LANES), 0) // blk
    klane = lax.broadcasted_iota(jnp.int32, (s_len, LANES), 1)
    k2 = k_ref[0]
    k_aug = (jnp.where(klane < HEAD_DIM, k2, jnp.where(klane - HEAD_DIM == kblk, 1.0, 0.0).astype(BF16)),
             jnp.where(klane < HEAD_DIM, jnp.where(klane == kblk, 1.0, 0.0).astype(BF16), k2))

    row = lax.broadcasted_iota(jnp.int32, (8, blk), 0)

    def drop_bias(s, j):
        out = jnp.zeros((8, blk), F32)
        past = row < j
        for n in range(j):
            rn = s[n:n + 1, :]
            beats = jnp.where(s > rn, 1.0, jnp.where(jnp.logical_and(s == rn, row < n), 1.0, 0.0))
            cnt = jnp.sum(jnp.where(past, beats, 0.0), axis=0, keepdims=True)
            out = jnp.where(row == n, jnp.where(cnt < float(MOBA_TOPK), 0.0, MASK_BIAS), out)
        return out

    def queries(j):
        q2 = q_ref[0, j * blk:(j + 1) * blk, :]
        qs = q2
        zero = jnp.zeros_like(qs)
        if j <= MOBA_TOPK:
            return jnp.where(first_head, qs, zero), jnp.where(first_head, zero, qs)
        s_t = (lax.dot_general(g_hi, q2, _NT, preferred_element_type=F32)
               + lax.dot_general(g_lo, q2, _NT, preferred_element_type=F32))
        padq = jnp.zeros((HEAD_DIM - 8, blk), F32)
        bias_t = jnp.concatenate([drop_bias(s_t[0:8], j), padq, drop_bias(s_t[HEAD_DIM:HEAD_DIM + 8], j), padq], axis=0)
        bias = bias_t.T.astype(BF16)
        return jnp.where(first_head, qs, bias), jnp.where(first_head, bias, qs)

    def scores(j):
        qa, qb = queries(j)
        width = (j + 1) * blk
        return (lax.dot_general(qa, k_aug[0][0:width], _NT, preferred_element_type=F32),
                lax.dot_general(qb, k_aug[1][0:width], _NT, preferred_element_type=F32))

    tri = lax.broadcasted_iota(jnp.int32, (blk, blk), 1) <= lax.broadcasted_iota(jnp.int32, (blk, blk), 0)

    v2 = v_ref[0]
    ones = jnp.ones_like(v2)
    v_aug = (jnp.where(klane < HEAD_DIM, v2, ones), jnp.where(klane < HEAD_DIM, ones, v2))
    sc = scores(0)
    for j in range(nb):
        nxt = scores(j + 1) if j + 1 < nb else None
        width = (j + 1) * blk
        outs = []
        for hd in range(2):
            parts = [sc[hd][:, n * blk:(n + 1) * blk] for n in range(j)]
            parts.append(jnp.where(tri, sc[hd][:, j * blk:width], NEG_BIG))
            m = jnp.max(parts[0], axis=1, keepdims=True)
            for pt in parts[1:]:
                m = jnp.maximum(m, jnp.max(pt, axis=1, keepdims=True))
            probs = jnp.concatenate([jnp.exp2(pt - m).astype(BF16) for pt in parts], axis=1)
            pv = _dot(probs, v_aug[hd][0:width])
            outs.append(pv / pltpu.roll(pv, HEAD_DIM, axis=1))
        o_ref[0, j * blk:(j + 1) * blk, :] = jnp.where(first_head, outs[0], outs[1]).astype(o_ref.dtype)
        sc = nxt


def _moba(qkv):
    b, s, w3 = qkv.shape
    blk = MOBA_BLOCK
    npair = ATT_HEADS // 2
    assert w3 == 3 * ATT_HEADS * HEAD_DIM and s % blk == 0 and s // blk <= 8
    nb = s // blk
    est = 2 * 4 * s * LANES * 2 + 4 * blk * s * (4 + 4 + 2)
    return pl.pallas_call(
        functools.partial(_moba_kernel, nb=nb),
        grid=(b, npair),
        in_specs=[pl.BlockSpec((1, s, LANES), lambda i, p: (i, 0, p)),
                  pl.BlockSpec((1, s, LANES), lambda i, p: (i, 0, npair + p)),
                  pl.BlockSpec((1, s, LANES), lambda i, p: (i, 0, 2 * npair + p))],
        out_specs=pl.BlockSpec((1, s, LANES), lambda i, p: (i, 0, p)),
        out_shape=jax.ShapeDtypeStruct((b, s, ATT_HEADS * HEAD_DIM), BF16),
        compiler_params=pltpu.CompilerParams(dimension_semantics=("arbitrary", "arbitrary"),
                                             vmem_limit_bytes=_vmem_limit(est)),
        name="moba",
    )(qkv, qkv, qkv)


_STG_KB, _STG_QN, _STG_KN, _STG_VB, _STG_KBE, _STG_QD = range(6)


def _gdn_kernel(x_ref, sm_ref, cw_ref, alog_ref, dtb_ref, ltri_ref, exp_ref, hsum_ref, o_ref,
                xs_ref, st_ref, stg_ref, gx_ref, *, rows, groups_per_seq):
    ck = DN_CHUNK
    dn_w = DN_HEADS * HEAD_DIM
    npair = DN_HEADS // 2
    t = pl.program_id(0)
    halo = 8

    @pl.when(t == 0)
    def _():
        xs_ref[0:halo, :] = jnp.zeros((halo, 3 * dn_w), F32)
        st_ref[...] = jnp.zeros_like(st_ref)
        stg_ref[...] = jnp.zeros_like(stg_ref)
        gx_ref[...] = jnp.zeros_like(gx_ref)

    ri = lax.broadcasted_iota(jnp.int32, (ck, dn_w), 0)
    ci = jnp.bitwise_and(lax.broadcasted_iota(jnp.int32, (ck, dn_w), 1), ck - 1)
    tri, strict, diag = ri >= ci, ri > ci, ri == ci
    r2 = lax.broadcasted_iota(jnp.int32, (LANES, LANES), 0) // HEAD_DIM
    c2 = lax.broadcasted_iota(jnp.int32, (LANES, LANES), 1) // HEAD_DIM
    bmask = r2 == c2
    eye2 = jnp.where(diag[:, :LANES], 1.0, 0.0)

    def bd2(a):
        a = a.astype(BF16)
        return jnp.where(bmask, jnp.concatenate([a, a], axis=0), jnp.zeros((LANES, LANES), BF16))

    nchunk = rows // ck
    chains = [(cc, p) for cc in range(nchunk) for p in range(npair)]
    rs = lambda cc: slice(cc * ck, (cc + 1) * ck)
    ls = lambda p: slice(p * LANES, (p + 1) * LANES)
    staged = lambda which, cc, p: stg_ref[which, rs(cc), ls(p)]

    gch = [gx_ref[rs(cc), :] for cc in range(nchunk)]
    glast = [g_[ck - 1:ck, :] for g_ in gch]
    dec, kdec, gl_exp = [], [], []
    for cc, g_ in enumerate(gch):
        grow = jnp.sum(jnp.where(diag, g_, 0.0), axis=0, keepdims=True)
        dec.append(jnp.where(tri, jnp.exp(jnp.where(tri, g_ - grow, 0.0)), 0.0))
        kdec.append((stg_ref[_STG_KN, rs(cc), :].astype(F32) * jnp.exp(glast[cc] - g_)).astype(BF16))
        gl_exp.append(jnp.exp(glast[cc]))
    qd_old = [stg_ref[_STG_QD, rs(cc), :] for cc in range(nchunk)]

    seq_start = lax.rem(t, groups_per_seq) == 0
    cw = cw_ref[...]
    hsum2 = hsum_ref[0:LANES, 0:LANES]
    act, sumsq = {}, {}

    def prepare(slab):
        cs = ls(slab)
        xs_ref[0:halo, cs] = jnp.where(seq_start, 0.0, xs_ref[0:halo, cs])
        xs_ref[halo:halo + rows, cs] = x_ref[0, :, cs].astype(F32)
        y = xs_ref[halo - 3:halo - 3 + rows, cs] * cw[0:1, cs]
        for j in range(1, CONV_WIDTH):
            y = y + xs_ref[halo - 3 + j:halo - 3 + j + rows, cs] * cw[j:j + 1, cs]
        xs_ref[0:halo, cs] = xs_ref[rows:rows + halo, cs]
        y = y * _sigmoid(y)
        act[slab] = y
        if slab < 2 * npair:
            sumsq[slab] = _dot((y * y).astype(BF16), hsum2)

    def stage(p):
        qn = act[p] * (lax.rsqrt(sumsq[p] + NORM_EPS) * (HEAD_DIM ** -0.5))
        kn = act[npair + p] * lax.rsqrt(sumsq[npair + p] + NORM_EPS)
        bx, eg = bexp[:, ls(p)], jnp.exp(gexp[:, ls(p)])
        kb = kn * bx
        stg_ref[_STG_KB, :, ls(p)] = kb.astype(BF16)
        stg_ref[_STG_QN, :, ls(p)] = qn.astype(BF16)
        stg_ref[_STG_KN, :, ls(p)] = kn.astype(BF16)
        stg_ref[_STG_VB, :, ls(p)] = (act[2 * npair + p] * bx).astype(BF16)
        stg_ref[_STG_KBE, :, ls(p)] = (kb * eg).astype(BF16)
        stg_ref[_STG_QD, :, ls(p)] = (qn * eg).astype(BF16)

    todo = list(range(3 * npair))

    def prepare_some(n):
        for _ in range(n):
            if todo:
                prepare(todo.pop(0))

    kq = {(cc, p): lax.dot_general(jnp.concatenate([staged(_STG_KB, cc, p), staged(_STG_QN, cc, p)], axis=0),
                                   bd2(staged(_STG_KN, cc, p)), _NT, preferred_element_type=F32)
          for cc, p in chains}
    vbd = {(cc, p): jnp.concatenate([bd2(staged(_STG_VB, cc, p)), bd2(staged(_STG_KBE, cc, p))], axis=1)
           for cc, p in chains}
    prepare_some(2)
    sm = sm_ref[0]
    xa = sm + dtb_ref[...]
    softplus = jnp.maximum(xa, 0.0) + jnp.log(1.0 + jnp.exp(-jnp.abs(xa)))
    g = -jnp.exp(alog_ref[...]) * softplus
    ltri = ltri_ref[...]
    gc = sum(_dot(ltri, t_) for t_ in _split3(g))
    neg_l = {(cc, p): -jnp.where(strict[:, ls(p)], kq[cc, p][:ck] * dec[cc][:, ls(p)], 0.0) for cc, p in chains}
    qk = {(cc, p): jnp.where(tri[:, ls(p)], kq[cc, p][ck:] * dec[cc][:, ls(p)], 0.0) for cc, p in chains}
    ssum = {ch: eye2 + neg_l[ch] for ch in chains}
    pw = {ch: _dot(neg_l[ch].astype(BF16), bd2(neg_l[ch])) for ch in chains}
    prepare_some(2)
    lane = lax.broadcasted_iota(jnp.int32, (rows, LANES), 1)
    comb = jnp.where(lane < DN_HEADS, _sigmoid(sm), gc)
    expd = exp_ref[...]
    ex = sum(_dot(t_, expd) for t_ in _split3(comb))
    bexp, gexp = ex[:, :dn_w], ex[:, dn_w:]
    span = 2
    while span * 2 < ck:
        both = {ch: _dot(jnp.concatenate([ssum[ch], pw[ch]], axis=0).astype(BF16), bd2(pw[ch])) for ch in chains}
        prepare_some(2)
        ssum = {ch: ssum[ch] + both[ch][:ck] for ch in chains}
        pw = {ch: both[ch][ck:] for ch in chains}
        span *= 2
    corr = {ch: _dot(ssum[ch].astype(BF16), bd2(pw[ch])) for ch in chains}
    prepare_some(len(todo))
    uw = {ch: _dot((ssum[ch] + corr[ch]).astype(BF16), vbd[ch]) for ch in chains}

    fresh = lax.rem(jnp.maximum(t - 1, 0), groups_per_seq) == 0
    state = [jnp.where(fresh, 0.0, st_ref[p]) for p in range(npair)]
    for cc in range(nchunk):
        wq = [_dot(jnp.concatenate([uw[cc, p][:, LANES:].astype(BF16), qd_old[cc][:, ls(p)]], axis=0),
                   state[p].astype(BF16)) for p in range(npair)]
        if cc < npair:
            stage(cc)
        vnew = [uw[cc, p][:, :LANES] - wq[p][:ck] for p in range(npair)]
        intra = [_dot(qk[cc, p].astype(BF16), bd2(vnew[p])) for p in range(npair)]
        upd = [lax.dot_general(kdec[cc][:, ls(p)], vnew[p].astype(BF16), _TN, preferred_element_type=F32)
               for p in range(npair)]
        for p in range(npair):
            o_ref[0, rs(cc), ls(p)] = (wq[p][ck:] + intra[p]).astype(o_ref.dtype)
        state = [state[p] * gl_exp[cc][:, ls(p)] + jnp.where(bmask, upd[p], 0.0) for p in range(npair)]
    for p in range(npair):
        st_ref[p] = state[p]
    for p in range(nchunk, npair):
        stage(p)
    gx_ref[...] = gexp


def _gdn(qkv_dn, small, conv_w, a_log, dt_bias, *, rows):
    b, s, w3 = qkv_dn.shape
    dn_w = DN_HEADS * HEAD_DIM
    assert w3 == 3 * dn_w and s % rows == 0 and rows % DN_CHUNK == 0
    h = DN_HEADS
    nc = s // rows
    nsteps = b * nc
    alog_row = jnp.zeros((1, LANES), F32).at[0, h:2 * h].set(a_log.astype(F32))
    dtb_row = jnp.zeros((1, LANES), F32).at[0, h:2 * h].set(dt_bias.astype(F32))
    r = jnp.arange(rows)
    ltri = ((r[:, None] // DN_CHUNK == r[None, :] // DN_CHUNK) & (r[None, :] <= r[:, None])).astype(BF16)
    src = jnp.arange(LANES)[:, None]
    dst = jnp.arange(2 * dn_w)[None, :]
    expander = ((src < 2 * h) & (dst // dn_w == src // h) & ((dst % dn_w) // HEAD_DIM == src % h)).astype(BF16)
    hl = jnp.arange(dn_w) // HEAD_DIM
    hsum = (hl[:, None] == hl[None, :]).astype(BF16)
    const = lambda a: pl.BlockSpec(a.shape, lambda t: (0,) * a.ndim)

    def produced(t):
        g = jnp.minimum(t, nsteps - 1)
        return g // nc, g % nc, 0

    def consumed(t):
        g = jnp.maximum(t - 1, 0)
        return g // nc, g % nc, 0

    est = (2 * (rows * w3 * 2 + rows * LANES * 4 + rows * dn_w * 4) + (rows + 8) * w3 * 4 + 12 * rows * w3 * 4
           + 6 * rows * dn_w * 2 + rows * dn_w * 4 + 2 * (ltri.size + expander.size + hsum.size) * 2)
    return pl.pallas_call(
        functools.partial(_gdn_kernel, rows=rows, groups_per_seq=nc),
        grid=(nsteps + 1,),
        in_specs=[pl.BlockSpec((1, rows, w3), produced),
                  pl.BlockSpec((1, rows, LANES), produced),
                  const(conv_w), const(alog_row), const(dtb_row), const(ltri), const(expander), const(hsum)],
        out_specs=pl.BlockSpec((1, rows, dn_w), consumed),
        out_shape=jax.ShapeDtypeStruct((b, s, dn_w), F32),
        scratch_shapes=[pltpu.VMEM((rows + 8, w3), F32), pltpu.VMEM((DN_HEADS // 2, LANES, LANES), F32),
                        pltpu.VMEM((6, rows, dn_w), BF16), pltpu.VMEM((rows, dn_w), F32)],
        compiler_params=pltpu.CompilerParams(dimension_semantics=("arbitrary",),
                                             vmem_limit_bytes=_vmem_limit(est)),
        name="gdn",
    )(qkv_dn, small, conv_w.astype(F32), alog_row, dtb_row, ltri, expander, hsum)


def _mixout_kernel(x_ref, ya_ref, od_ref, z_ref, gate_ref, dnw_ref, hmean_ref, wa_ref, wd_ref, wo_ref, pn_ref, o_ref):
    d = x_ref.shape[1]
    od = od_ref[...]
    ms = _dot((od * od).astype(BF16), hmean_ref[...])
    z = z_ref[...].astype(F32)
    y_dn = od * lax.rsqrt(ms + NORM_EPS) * dnw_ref[...] * (z * _sigmoid(z))
    ga = _sigmoid(gate_ref[:, :d].astype(F32))
    gd = _sigmoid(gate_ref[:, d:].astype(F32))
    merged = ga * _dot(ya_ref[...], wa_ref[...]) + gd * _dot(y_dn.astype(BF16), wd_ref[...])
    y = _dot(merged.astype(BF16), wo_ref[...])
    o_ref[...] = x_ref[...] + _rms(y, pn_ref[...])


def _mixout(x2, y_att, o_dn, z, gates, dn_norm, wa, wd, wo, post_norm, *, tm):
    t, d = x2.shape
    dn_w = DN_HEADS * HEAD_DIM
    dnw_row = jnp.tile(dn_norm.astype(F32), DN_HEADS)[None, :]
    hl = jnp.arange(dn_w) // HEAD_DIM
    hmean = ((hl[:, None] == hl[None, :]).astype(F32) / HEAD_DIM).astype(BF16)
    row = lambda w: pl.BlockSpec((tm, w), lambda i: (i, 0))
    const = lambda a: pl.BlockSpec(a.shape, lambda i: (0, 0))
    est = (2 * (2 * tm * d * 4 + tm * dn_w * (2 + 4 + 2) + tm * 2 * d * 2)
           + 2 * (hmean.size + wa.size + wd.size + wo.size) * 2 + 8 * tm * d * 4)
    return pl.pallas_call(
        _mixout_kernel,
        grid=(t // tm,),
        in_specs=[row(d), row(dn_w), row(dn_w), row(dn_w), row(2 * d), const(dnw_row), const(hmean),
                  const(wa), const(wd), const(wo), const(post_norm)],
        out_specs=row(d),
        out_shape=jax.ShapeDtypeStruct((t, d), F32),
        compiler_params=pltpu.CompilerParams(dimension_semantics=("arbitrary",), vmem_limit_bytes=_vmem_limit(est)),
        name="mixout",
    )(x2, y_att, o_dn, z, gates, dnw_row, hmean, wa, wd, wo, post_norm)


def _mlp_kernel(x_ref, pre_ref, w1_ref, w2_ref, post_ref, o_ref, *, ff_chunk):
    x = x_ref[...]
    hb = _rms(x, pre_ref[...]).astype(BF16)
    acc = jnp.zeros(x.shape, F32)
    for c in range(0, w1_ref.shape[1], ff_chunk):
        a = jnp.maximum(_dot(hb, w1_ref[:, c:c + ff_chunk]), 0.0)
        acc = acc + _dot((a * a).astype(BF16), w2_ref[c:c + ff_chunk, :])
    o_ref[...] = x + _rms(acc, post_ref[...])


def _mlp(x1, pre, w1, w2, post, *, tm):
    t, d = x1.shape
    row = pl.BlockSpec((tm, d), lambda i: (i, 0))
    const = lambda a: pl.BlockSpec(a.shape, lambda i: (0, 0))
    est = 2 * (2 * tm * d * 4 + (w1.size + w2.size) * 2) + 6 * tm * d * 4 + 2 * tm * 1024 * 4
    return pl.pallas_call(
        functools.partial(_mlp_kernel, ff_chunk=1024),
        grid=(t // tm,),
        in_specs=[row, const(pre), const(w1), const(w2), const(post)],
        out_specs=row,
        out_shape=jax.ShapeDtypeStruct((t, d), F32),
        compiler_params=pltpu.CompilerParams(dimension_semantics=("arbitrary",), vmem_limit_bytes=_vmem_limit(est)),
        name="mlp",
    )(x1, pre, w1, w2, post)


def kernel(x, pre_norm_mix, w_in, conv_w, a_log, dt_bias, dn_norm, w_branch_att, w_branch_dn,
           w_out, post_norm_mix, pre_norm_mlp, w_mlp_in, w_mlp_out, post_norm_mlp):
    b, s, d = x.shape
    att_w = ATT_HEADS * HEAD_DIM
    dn_w = DN_HEADS * HEAD_DIM
    n_main = 3 * att_w + 3 * dn_w + dn_w
    tm = 512
    x2 = x.reshape(b * s, d)
    for l in range(w_in.shape[0]):
        wl = w_in[l]
        q_scale = HEAD_DIM ** -0.5 * LOG2_E
        w_main = jnp.concatenate([wl[:, :att_w] * q_scale, wl[:, att_w:n_main], wl[:, n_main + 2 * DN_HEADS:]],
                                 axis=1).astype(BF16)
        w_small = jnp.pad(wl[:, n_main:n_main + 2 * DN_HEADS], ((0, 0), (0, LANES - 2 * DN_HEADS))).astype(BF16)
        qkv_att, qkv_dn, z, gates, small = _inproj(x2, pre_norm_mix[l][None, :], w_main, w_small, tm=tm)
        y_att = _moba(qkv_att.reshape(b, s, 3 * att_w))
        o_dn = _gdn(qkv_dn.reshape(b, s, 3 * dn_w), small.reshape(b, s, LANES), conv_w[l], a_log[l], dt_bias[l], rows=256)
        x2 = _mixout(x2, y_att.reshape(b * s, att_w), o_dn.reshape(b * s, dn_w), z, gates, dn_norm[l],
                     w_branch_att[l].astype(BF16), w_branch_dn[l].astype(BF16), w_out[l].astype(BF16),
                     post_norm_mix[l][None, :], tm=tm)
        x2 = _mlp(x2, pre_norm_mlp[l][None, :], w_mlp_in[l].astype(BF16), w_mlp_out[l].astype(BF16),
                  post_norm_mlp[l][None, :], tm=tm)
    return x2.reshape(b, s, d)
```

```python
import functools

import jax
import jax.numpy as jnp
from jax import lax
from jax.experimental import pallas as pl
from jax.experimental.pallas import tpu as pltpu

ATT_HEADS = 8
DN_HEADS = 8
HEAD_DIM = 64
MOBA_BLOCK = 256
MOBA_TOPK = 3
DN_CHUNK = 64
CONV_WIDTH = 4
NORM_EPS = 1e-6

LANES = 128
V7X_VMEM_BYTES = 64 * 1024 * 1024
VMEM_CAP_BYTES = 56 * 1024 * 1024

F32 = jnp.float32
BF16 = jnp.bfloat16
NEG_BIG = -1e30
MASK_BIAS = -(2.0 ** 100)
LOG2_E = 1.4426950408889634

_NT = (((1,), (1,)), ((), ()))
_TN = (((0,), (0,)), ((), ()))


def _vmem_limit(nbytes):
    return int(min(VMEM_CAP_BYTES, nbytes * 5 // 4 + (4 << 20)))


def _dot(a, b):
    return jnp.dot(a, b, preferred_element_type=F32)


def _rms(x, w):
    return x * lax.rsqrt(jnp.mean(x * x, axis=-1, keepdims=True) + NORM_EPS) * w


def _sigmoid(x):
    return 1.0 / (1.0 + jnp.exp(-x))


def _split3(x):
    hi = x.astype(BF16)
    r = x - hi.astype(F32)
    mid = r.astype(BF16)
    lo = (r - mid.astype(F32)).astype(BF16)
    return hi, mid, lo


def _inproj_kernel(x_ref, g_ref, wm_ref, ws_ref, att_ref, dn_ref, z_ref, gate_ref, small_ref, *, col_chunk):
    hb = _rms(x_ref[...], g_ref[...]).astype(BF16)
    start = 0
    for ref in (att_ref, dn_ref, z_ref, gate_ref):
        width = ref.shape[1]
        for c in range(0, width, col_chunk):
            ref[:, c:c + col_chunk] = _dot(hb, wm_ref[:, start + c:start + c + col_chunk]).astype(ref.dtype)
        start += width
    small_ref[...] = lax.dot_general(ws_ref[...], hb, _NT, preferred_element_type=F32)


def _inproj(x2, gain, w_main, w_small, *, tm):
    t, d = x2.shape
    att_w, dn_w, z_w, gate_w = 3 * ATT_HEADS * HEAD_DIM, 3 * DN_HEADS * HEAD_DIM, DN_HEADS * HEAD_DIM, 2 * d
    assert w_main.shape == (d, att_w + dn_w + z_w + gate_w) and t % tm == 0
    row = lambda w: pl.BlockSpec((tm, w), lambda i: (i, 0))
    const = lambda shp: pl.BlockSpec(shp, lambda i: (0, 0))
    est = 2 * (tm * d * 4 + w_main.size * 2 + w_small.size * 2 + tm * w_main.shape[1] * 2 + tm * LANES * 4) + 4 * tm * d * 4
    return pl.pallas_call(
        functools.partial(_inproj_kernel, col_chunk=512),
        grid=(t // tm,),
        in_specs=[row(d), const((1, d)), const(w_main.shape), const(w_small.shape)],
        out_specs=[row(att_w), row(dn_w), row(z_w), row(gate_w), pl.BlockSpec((w_small.shape[0], tm), lambda i: (0, i))],
        out_shape=[jax.ShapeDtypeStruct((t, att_w), BF16), jax.ShapeDtypeStruct((t, dn_w), BF16),
                   jax.ShapeDtypeStruct((t, z_w), BF16), jax.ShapeDtypeStruct((t, gate_w), BF16),
                   jax.ShapeDtypeStruct((w_small.shape[0], t), F32)],
        compiler_params=pltpu.CompilerParams(dimension_semantics=("arbitrary",), vmem_limit_bytes=_vmem_limit(est)),
        name="inproj",
    )(x2, gain, w_main, w_small)


def _moba_kernel(q_ref, k_ref, v_ref, o_ref, *, nb):
    blk = MOBA_BLOCK
    lane = lax.broadcasted_iota(jnp.int32, (blk, LANES), 1)
    first_head = lane < HEAD_DIM

    km = jnp.concatenate(
        [jnp.sum(k_ref[0, n * blk:(n + 1) * blk, :].astype(F32), axis=0, keepdims=True) for n in range(nb)]
        + [jnp.zeros((8 - nb, LANES), F32)] * (1 if nb < 8 else 0), axis=0) * (1.0 / blk)
    l8 = lax.broadcasted_iota(jnp.int32, (8, LANES), 1) < HEAD_DIM
    pad = jnp.zeros((HEAD_DIM - 8, LANES), F32)
    g = jnp.concatenate([jnp.where(l8, 0.0, km), pad, jnp.where(l8, km, 0.0), pad], axis=0)
    g_hi = g.astype(BF16)
    g_lo = (g - g_hi.astype(F32)).astype(BF16)

    s_len = nb * blk
    kblk = lax.broadcasted_iota(jnp.int32, (s_len, LANES), 0) // blk
    klane = lax.broadcasted_iota(jnp.int32, (s_len, LANES), 1)
    k2 = k_ref[0]
    k_aug = (jnp.where(klane < HEAD_DIM, k2, jnp.where(klane - HEAD_DIM == kblk, 1.0, 0.0).astype(BF16)),
             jnp.where(klane < HEAD_DIM, jnp.where(klane == kblk, 1.0, 0.0).astype(BF16), k2))

    row = lax.broadcasted_iota(jnp.int32, (8, blk), 0)

    def drop_bias(s, j):
        out = jnp.zeros((8, blk), F32)
        past = row < j
        for n in range(j):
            rn = s[n:n + 1, :]
            beats = jnp.where(s > rn, 1.0, jnp.where(jnp.logical_and(s == rn, row < n), 1.0, 0.0))
            cnt = jnp.sum(jnp.where(past, beats, 0.0), axis=0, keepdims=True)
            out = jnp.where(row == n, jnp.where(cnt < float(MOBA_TOPK), 0.0, MASK_BIAS), out)
        return out

    def queries(j):
        q2 = q_ref[0, j * blk:(j + 1) * blk, :]
        qs = q2
        zero = jnp.zeros_like(qs)
        if j <= MOBA_TOPK:
            return jnp.where(first_head, qs, zero), jnp.where(first_head, zero, qs)
        s_t = (lax.dot_general(g_hi, q2, _NT, preferred_element_type=F32)
               + lax.dot_general(g_lo, q2, _NT, preferred_element_type=F32))
        padq = jnp.zeros((HEAD_DIM - 8, blk), F32)
        bias_t = jnp.concatenate([drop_bias(s_t[0:8], j), padq, drop_bias(s_t[HEAD_DIM:HEAD_DIM + 8], j), padq], axis=0)
        bias = bias_t.T.astype(BF16)
        return jnp.where(first_head, qs, bias), jnp.where(first_head, bias, qs)

    def scores(j):
        qa, qb = queries(j)
        width = (j + 1) * blk
        return (lax.dot_general(qa, k_aug[0][0:width], _NT, preferred_element_type=F32),
                lax.dot_general(qb, k_aug[1][0:width], _NT, preferred_element_type=F32))

    tri = lax.broadcasted_iota(jnp.int32, (blk, blk), 1) <= lax.broadcasted_iota(jnp.int32, (blk, blk), 0)

    v2 = v_ref[0]
    ones = jnp.ones_like(v2)
    v_aug = (jnp.where(klane < HEAD_DIM, v2, ones), jnp.where(klane < HEAD_DIM, ones, v2))
    sc = scores(0)
    for j in range(nb):
        nxt = scores(j + 1) if j + 1 < nb else None
        width = (j + 1) * blk
        outs = []
        for hd in range(2):
            parts = [sc[hd][:, n * blk:(n + 1) * blk] for n in range(j)]
            parts.append(jnp.where(tri, sc[hd][:, j * blk:width], NEG_BIG))
            m = jnp.max(parts[0], axis=1, keepdims=True)
            for pt in parts[1:]:
                m = jnp.maximum(m, jnp.max(pt, axis=1, keepdims=True))
            probs = jnp.concatenate([jnp.exp2(pt - m).astype(BF16) for pt in parts], axis=1)
            pv = _dot(probs, v_aug[hd][0:width])
            outs.append(pv / pltpu.roll(pv, HEAD_DIM, axis=1))
        o_ref[0, j * blk:(j + 1) * blk, :] = jnp.where(first_head, outs[0], outs[1]).astype(o_ref.dtype)
        sc = nxt


def _moba(qkv):
    b, s, w3 = qkv.shape
    blk = MOBA_BLOCK
    npair = ATT_HEADS // 2
    assert w3 == 3 * ATT_HEADS * HEAD_DIM and s % blk == 0 and s // blk <= 8
    nb = s // blk
    est = 2 * 4 * s * LANES * 2 + 4 * blk * s * (4 + 4 + 2)
    return pl.pallas_call(
        functools.partial(_moba_kernel, nb=nb),
        grid=(b, npair),
        in_specs=[pl.BlockSpec((1, s, LANES), lambda i, p: (i, 0, p)),
                  pl.BlockSpec((1, s, LANES), lambda i, p: (i, 0, npair + p)),
                  pl.BlockSpec((1, s, LANES), lambda i, p: (i, 0, 2 * npair + p))],
        out_specs=pl.BlockSpec((1, s, LANES), lambda i, p: (i, 0, p)),
        out_shape=jax.ShapeDtypeStruct((b, s, ATT_HEADS * HEAD_DIM), BF16),
        compiler_params=pltpu.CompilerParams(dimension_semantics=("arbitrary", "arbitrary"),
                                             vmem_limit_bytes=_vmem_limit(est)),
        name="moba",
    )(qkv, qkv, qkv)


_STG_KB, _STG_QN, _STG_KN, _STG_VB, _STG_KBE, _STG_QD = range(6)


def _gdn_kernel(x_ref, sm_ref, cw_ref, alog_ref, dtb_ref, utri_ref, exp_ref, hsum_ref, o_ref,
                xs_ref, st_ref, stg_ref, gx_ref, *, rows, groups_per_seq):
    ck = DN_CHUNK
    dn_w = DN_HEADS * HEAD_DIM
    npair = DN_HEADS // 2
    t = pl.program_id(0)
    halo = 8

    @pl.when(t == 0)
    def _():
        xs_ref[0:halo, :] = jnp.zeros((halo, 3 * dn_w), F32)
        st_ref[...] = jnp.zeros_like(st_ref)
        stg_ref[...] = jnp.zeros_like(stg_ref)
        gx_ref[...] = jnp.zeros_like(gx_ref)

    ri = lax.broadcasted_iota(jnp.int32, (ck, dn_w), 0)
    ci = jnp.bitwise_and(lax.broadcasted_iota(jnp.int32, (ck, dn_w), 1), ck - 1)
    tri, strict, diag = ri >= ci, ri > ci, ri == ci
    r2 = lax.broadcasted_iota(jnp.int32, (LANES, LANES), 0) // HEAD_DIM
    c2 = lax.broadcasted_iota(jnp.int32, (LANES, LANES), 1) // HEAD_DIM
    bmask = r2 == c2
    eye2 = jnp.where(diag[:, :LANES], 1.0, 0.0)

    def bd2(a):
        a = a.astype(BF16)
        return jnp.where(bmask, jnp.concatenate([a, a], axis=0), jnp.zeros((LANES, LANES), BF16))

    nchunk = rows // ck
    chains = [(cc, p) for cc in range(nchunk) for p in range(npair)]
    rs = lambda cc: slice(cc * ck, (cc + 1) * ck)
    ls = lambda p: slice(p * LANES, (p + 1) * LANES)
    staged = lambda which, cc, p: stg_ref[which, rs(cc), ls(p)]

    gch = [gx_ref[rs(cc), :] for cc in range(nchunk)]
    glast = [g_[ck - 1:ck, :] for g_ in gch]
    dec, kdec, gl_exp = [], [], []
    for cc, g_ in enumerate(gch):
        grow = jnp.sum(jnp.where(diag, g_, 0.0), axis=0, keepdims=True)
        dec.append(jnp.where(tri, jnp.exp(jnp.where(tri, g_ - grow, 0.0)), 0.0))
        kdec.append((stg_ref[_STG_KN, rs(cc), :].astype(F32) * jnp.exp(glast[cc] - g_)).astype(BF16))
        gl_exp.append(jnp.exp(glast[cc]))
    qd_old = [stg_ref[_STG_QD, rs(cc), :] for cc in range(nchunk)]

    seq_start = lax.rem(t, groups_per_seq) == 0
    cw = cw_ref[...]
    hsum2 = hsum_ref[0:LANES, 0:LANES]
    act, sumsq = {}, {}

    def prepare(slab):
        cs = ls(slab)
        xs_ref[0:halo, cs] = jnp.where(seq_start, 0.0, xs_ref[0:halo, cs])
        xs_ref[halo:halo + rows, cs] = x_ref[0, :, cs].astype(F32)
        xf = xs_ref[:, cs]
        y = xf * cw[CONV_WIDTH - 1:CONV_WIDTH, cs]
        for j in range(1, CONV_WIDTH):
            y = y + pltpu.roll(xf, j, axis=0) * cw[CONV_WIDTH - 1 - j:CONV_WIDTH - j, cs]
        y = y[halo:]
        xs_ref[0:halo, cs] = xf[rows:]
        y = y * _sigmoid(y)
        act[slab] = y
        if slab < 2 * npair:
            sumsq[slab] = _dot((y * y).astype(BF16), hsum2)

    def stage(p):
        qn = act[p] * (lax.rsqrt(sumsq[p] + NORM_EPS) * (HEAD_DIM ** -0.5))
        kn = act[npair + p] * lax.rsqrt(sumsq[npair + p] + NORM_EPS)
        bx, eg = bexp[:, ls(p)], jnp.exp(gexp[:, ls(p)])
        kb = kn * bx
        stg_ref[_STG_KB, :, ls(p)] = kb.astype(BF16)
        stg_ref[_STG_QN, :, ls(p)] = qn.astype(BF16)
        stg_ref[_STG_KN, :, ls(p)] = kn.astype(BF16)
        stg_ref[_STG_VB, :, ls(p)] = (act[2 * npair + p] * bx).astype(BF16)
        stg_ref[_STG_KBE, :, ls(p)] = (kb * eg).astype(BF16)
        stg_ref[_STG_QD, :, ls(p)] = (qn * eg).astype(BF16)

    todo = list(range(3 * npair))

    def prepare_some(n):
        for _ in range(n):
            if todo:
                prepare(todo.pop(0))

    kq = {(cc, p): lax.dot_general(jnp.concatenate([staged(_STG_KB, cc, p), staged(_STG_QN, cc, p)], axis=0),
                                   bd2(staged(_STG_KN, cc, p)), _NT, preferred_element_type=F32)
          for cc, p in chains}
    vbd = {(cc, p): jnp.concatenate([bd2(staged(_STG_VB, cc, p)), bd2(staged(_STG_KBE, cc, p))], axis=1)
           for cc, p in chains}
    sm = sm_ref[...]
    widen = lambda a: jnp.concatenate([a] * (rows // LANES), axis=1)
    xa = sm + widen(dtb_ref[...])
    softplus = jnp.maximum(xa, 0.0) + jnp.log(1.0 + jnp.exp(-jnp.abs(xa)))
    g = -jnp.exp(widen(alog_ref[...])) * softplus
    utri = utri_ref[...]
    gc = sum(_dot(t_, utri) for t_ in _split3(g))
    prepare_some(2)
    neg_l = {(cc, p): -jnp.where(strict[:, ls(p)], kq[cc, p][:ck] * dec[cc][:, ls(p)], 0.0) for cc, p in chains}
    qk = {(cc, p): jnp.where(tri[:, ls(p)], kq[cc, p][ck:] * dec[cc][:, ls(p)], 0.0) for cc, p in chains}
    ssum = {ch: eye2 + neg_l[ch] for ch in chains}
    pw = {ch: _dot(neg_l[ch].astype(BF16), bd2(neg_l[ch])) for ch in chains}
    gate_row = lax.broadcasted_iota(jnp.int32, (2 * DN_HEADS, rows), 0)
    comb = jnp.where(gate_row < DN_HEADS, _sigmoid(sm), gc)
    ex = lax.dot_general(jnp.concatenate(_split3(comb), axis=0), exp_ref[...], _TN, preferred_element_type=F32)
    bexp, gexp = ex[:, :dn_w], ex[:, dn_w:]
    prepare_some(2)
    span = 2
    while span * 2 < ck:
        both = {ch: _dot(jnp.concatenate([ssum[ch], pw[ch]], axis=0).astype(BF16), bd2(pw[ch])) for ch in chains}
        prepare_some(2)
        ssum = {ch: ssum[ch] + both[ch][:ck] for ch in chains}
        pw = {ch: both[ch][ck:] for ch in chains}
        span *= 2
    corr = {ch: _dot(ssum[ch].astype(BF16), bd2(pw[ch])) for ch in chains}
    prepare_some(len(todo))
    uw = {ch: _dot((ssum[ch] + corr[ch]).astype(BF16), vbd[ch]) for ch in chains}

    fresh = lax.rem(jnp.maximum(t - 1, 0), groups_per_seq) == 0
    state = [jnp.where(fresh, 0.0, st_ref[p]) for p in range(npair)]
    for cc in range(nchunk):
        wq = [_dot(jnp.concatenate([uw[cc, p][:, LANES:].astype(BF16), qd_old[cc][:, ls(p)]], axis=0),
                   state[p].astype(BF16)) for p in range(npair)]
        if cc < npair:
            stage(cc)
        vnew = [uw[cc, p][:, :LANES] - wq[p][:ck] for p in range(npair)]
        intra = [_dot(qk[cc, p].astype(BF16), bd2(vnew[p])) for p in range(npair)]
        upd = [lax.dot_general(kdec[cc][:, ls(p)], vnew[p].astype(BF16), _TN, preferred_element_type=F32)
               for p in range(npair)]
        for p in range(npair):
            o_ref[0, rs(cc), ls(p)] = (wq[p][ck:] + intra[p]).astype(o_ref.dtype)
        state = [state[p] * gl_exp[cc][:, ls(p)] + jnp.where(bmask, upd[p], 0.0) for p in range(npair)]
    for p in range(npair):
        st_ref[p] = state[p]
    for p in range(nchunk, npair):
        stage(p)
    gx_ref[...] = gexp


def _gdn(qkv_dn, small, conv_w, a_log, dt_bias, *, rows):
    b, s, w3 = qkv_dn.shape
    dn_w = DN_HEADS * HEAD_DIM
    assert w3 == 3 * dn_w and s % rows == 0 and rows % DN_CHUNK == 0
    h = DN_HEADS
    nc = s // rows
    nsteps = b * nc
    assert small.shape == (2 * h, b * s) and rows % LANES == 0
    zeros = jnp.zeros((h,), F32)
    alog_col = jnp.tile(jnp.concatenate([zeros, a_log.astype(F32)])[:, None], (1, LANES))
    dtb_col = jnp.tile(jnp.concatenate([zeros, dt_bias.astype(F32)])[:, None], (1, LANES))
    r = jnp.arange(rows)
    utri = ((r[:, None] // DN_CHUNK == r[None, :] // DN_CHUNK) & (r[:, None] <= r[None, :])).astype(BF16)
    src = jnp.arange(2 * h)[:, None]
    dst = jnp.arange(2 * dn_w)[None, :]
    expander = ((dst // dn_w == src // h) & ((dst % dn_w) // HEAD_DIM == src % h)).astype(BF16)
    expander = jnp.tile(expander, (3, 1))
    hl = jnp.arange(dn_w) // HEAD_DIM
    hsum = (hl[:, None] == hl[None, :]).astype(BF16)
    const = lambda a: pl.BlockSpec(a.shape, lambda t: (0,) * a.ndim)

    def produced(t):
        g = jnp.minimum(t, nsteps - 1)
        return g // nc, g % nc, 0

    def consumed(t):
        g = jnp.maximum(t - 1, 0)
        return g // nc, g % nc, 0

    est = (2 * (rows * w3 * 2 + 2 * h * rows * 4 + rows * dn_w * 4) + (rows + 8) * w3 * 4 + 12 * rows * w3 * 4
           + 6 * rows * dn_w * 2 + rows * dn_w * 4 + 2 * (utri.size + expander.size + hsum.size) * 2)
    return pl.pallas_call(
        functools.partial(_gdn_kernel, rows=rows, groups_per_seq=nc),
        grid=(nsteps + 1,),
        in_specs=[pl.BlockSpec((1, rows, w3), produced),
                  pl.BlockSpec((2 * h, rows), lambda t: (0, jnp.minimum(t, nsteps - 1))),
                  const(conv_w), const(alog_col), const(dtb_col), const(utri), const(expander), const(hsum)],
        out_specs=pl.BlockSpec((1, rows, dn_w), consumed),
        out_shape=jax.ShapeDtypeStruct((b, s, dn_w), F32),
        scratch_shapes=[pltpu.VMEM((rows + 8, w3), F32), pltpu.VMEM((DN_HEADS // 2, LANES, LANES), F32),
                        pltpu.VMEM((6, rows, dn_w), BF16), pltpu.VMEM((rows, dn_w), F32)],
        compiler_params=pltpu.CompilerParams(dimension_semantics=("arbitrary",),
                                             vmem_limit_bytes=_vmem_limit(est)),
        name="gdn",
    )(qkv_dn, small, conv_w.astype(F32), alog_col, dtb_col, utri, expander, hsum)


def _mixout_kernel(x_ref, ya_ref, od_ref, z_ref, gate_ref, dnw_ref, hmean_ref, wa_ref, wd_ref, wo_ref, pn_ref, o_ref):
    d = x_ref.shape[1]
    od = od_ref[...]
    ms = _dot((od * od).astype(BF16), hmean_ref[...])
    z = z_ref[...].astype(F32)
    y_dn = od * lax.rsqrt(ms + NORM_EPS) * dnw_ref[...] * (z * _sigmoid(z))
    ga = _sigmoid(gate_ref[:, :d].astype(F32))
    gd = _sigmoid(gate_ref[:, d:].astype(F32))
    merged = ga * _dot(ya_ref[...], wa_ref[...]) + gd * _dot(y_dn.astype(BF16), wd_ref[...])
    y = _dot(merged.astype(BF16), wo_ref[...])
    o_ref[...] = x_ref[...] + _rms(y, pn_ref[...])


def _mixout(x2, y_att, o_dn, z, gates, dn_norm, wa, wd, wo, post_norm, *, tm):
    t, d = x2.shape
    dn_w = DN_HEADS * HEAD_DIM
    dnw_row = jnp.tile(dn_norm.astype(F32), DN_HEADS)[None, :]
    hl = jnp.arange(dn_w) // HEAD_DIM
    hmean = ((hl[:, None] == hl[None, :]).astype(F32) / HEAD_DIM).astype(BF16)
    row = lambda w: pl.BlockSpec((tm, w), lambda i: (i, 0))
    const = lambda a: pl.BlockSpec(a.shape, lambda i: (0, 0))
    est = (2 * (2 * tm * d * 4 + tm * dn_w * (2 + 4 + 2) + tm * 2 * d * 2)
           + 2 * (hmean.size + wa.size + wd.size + wo.size) * 2 + 8 * tm * d * 4)
    return pl.pallas_call(
        _mixout_kernel,
        grid=(t // tm,),
        in_specs=[row(d), row(dn_w), row(dn_w), row(dn_w), row(2 * d), const(dnw_row), const(hmean),
                  const(wa), const(wd), const(wo), const(post_norm)],
        out_specs=row(d),
        out_shape=jax.ShapeDtypeStruct((t, d), F32),
        compiler_params=pltpu.CompilerParams(dimension_semantics=("arbitrary",), vmem_limit_bytes=_vmem_limit(est)),
        name="mixout",
    )(x2, y_att, o_dn, z, gates, dnw_row, hmean, wa, wd, wo, post_norm)


def _mlp_kernel(x_ref, pre_ref, w1_ref, w2_ref, post_ref, o_ref, *, ff_chunk):
    x = x_ref[...]
    hb = _rms(x, pre_ref[...]).astype(BF16)
    acc = jnp.zeros(x.shape, F32)
    for c in range(0, w1_ref.shape[1], ff_chunk):
        a = jnp.maximum(_dot(hb, w1_ref[:, c:c + ff_chunk]), 0.0)
        acc = acc + _dot((a * a).astype(BF16), w2_ref[c:c + ff_chunk, :])
    o_ref[...] = x + _rms(acc, post_ref[...])


def _mlp(x1, pre, w1, w2, post, *, tm):
    t, d = x1.shape
    row = pl.BlockSpec((tm, d), lambda i: (i, 0))
    const = lambda a: pl.BlockSpec(a.shape, lambda i: (0, 0))
    est = 2 * (2 * tm * d * 4 + (w1.size + w2.size) * 2) + 6 * tm * d * 4 + 2 * tm * 1024 * 4
    return pl.pallas_call(
        functools.partial(_mlp_kernel, ff_chunk=1024),
        grid=(t // tm,),
        in_specs=[row, const(pre), const(w1), const(w2), const(post)],
        out_specs=row,
        out_shape=jax.ShapeDtypeStruct((t, d), F32),
        compiler_params=pltpu.CompilerParams(dimension_semantics=("arbitrary",), vmem_limit_bytes=_vmem_limit(est)),
        name="mlp",
    )(x1, pre, w1, w2, post)


def kernel(x, pre_norm_mix, w_in, conv_w, a_log, dt_bias, dn_norm, w_branch_att, w_branch_dn,
           w_out, post_norm_mix, pre_norm_mlp, w_mlp_in, w_mlp_out, post_norm_mlp):
    b, s, d = x.shape
    att_w = ATT_HEADS * HEAD_DIM
    dn_w = DN_HEADS * HEAD_DIM
    n_main = 3 * att_w + 3 * dn_w + dn_w
    tm = 512
    x2 = x.reshape(b * s, d)
    for l in range(w_in.shape[0]):
        wl = w_in[l]
        q_scale = HEAD_DIM ** -0.5 * LOG2_E
        w_main = jnp.concatenate([wl[:, :att_w] * q_scale, wl[:, att_w:n_main], wl[:, n_main + 2 * DN_HEADS:]],
                                 axis=1).astype(BF16)
        w_small = wl[:, n_main:n_main + 2 * DN_HEADS].T.astype(BF16)
        qkv_att, qkv_dn, z, gates, small = _inproj(x2, pre_norm_mix[l][None, :], w_main, w_small, tm=tm)
        y_att = _moba(qkv_att.reshape(b, s, 3 * att_w))
        o_dn = _gdn(qkv_dn.reshape(b, s, 3 * dn_w), small, conv_w[l], a_log[l], dt_bias[l], rows=256)
        x2 = _mixout(x2, y_att.reshape(b * s, att_w), o_dn.reshape(b * s, dn_w), z, gates, dn_norm[l],
                     w_branch_att[l].astype(BF16), w_branch_dn[l].astype(BF16), w_out[l].astype(BF16),
                     post_norm_mix[l][None, :], tm=tm)
        x2 = _mlp(x2, pre_norm_mlp[l][None, :], w_mlp_in[l].astype(BF16), w_mlp_out[l].astype(BF16),
                  post_norm_mlp[l][None, :], tm=tm)
    return x2.reshape(b, s, d)
```

```python
import functools

import jax
import jax.numpy as jnp
from jax import lax
from jax.experimental import pallas as pl
from jax.experimental.pallas import tpu as pltpu

ATT_HEADS = 8
DN_HEADS = 8
HEAD_DIM = 64
MOBA_BLOCK = 256
MOBA_TOPK = 3
DN_CHUNK = 64
CONV_WIDTH = 4
NORM_EPS = 1e-6

LANES = 128
V7X_VMEM_BYTES = 64 * 1024 * 1024
VMEM_CAP_BYTES = V7X_VMEM_BYTES - 8 * 1024 * 1024

F32 = jnp.float32
BF16 = jnp.bfloat16
NEG_BIG = -1e30
MASK_BIAS = -(2.0 ** 100)
LOG2_E = 1.4426950408889634

_NT = (((1,), (1,)), ((), ()))
_TN = (((0,), (0,)), ((), ()))


def _vmem_limit(nbytes):
    return int(min(VMEM_CAP_BYTES, nbytes * 5 // 4 + (4 << 20)))


def _dot(a, b):
    return jnp.dot(a, b, preferred_element_type=F32)


def _rms(x, w):
    return x * lax.rsqrt(jnp.mean(x * x, axis=-1, keepdims=True) + NORM_EPS) * w


def _sigmoid(x):
    return 1.0 / (1.0 + jnp.exp(-x))


def _split3(x):
    hi = x.astype(BF16)
    r = x - hi.astype(F32)
    mid = r.astype(BF16)
    lo = (r - mid.astype(F32)).astype(BF16)
    return hi, mid, lo


def _inproj_kernel(x_ref, g_ref, wm_ref, ws_ref, att_ref, dn_ref, z_ref, gate_ref, small_ref, *, col_chunk):
    hb = _rms(x_ref[...], g_ref[...]).astype(BF16)
    start = 0
    for ref in (att_ref, dn_ref, z_ref, gate_ref):
        width = ref.shape[1]
        for c in range(0, width, col_chunk):
            ref[:, c:c + col_chunk] = _dot(hb, wm_ref[:, start + c:start + c + col_chunk]).astype(ref.dtype)
        start += width
    small_ref[...] = lax.dot_general(ws_ref[...], hb, _NT, preferred_element_type=F32)


def _inproj(x2, gain, w_main, w_small, *, tm):
    t, d = x2.shape
    att_w, dn_w, z_w, gate_w = 3 * ATT_HEADS * HEAD_DIM, 3 * DN_HEADS * HEAD_DIM, DN_HEADS * HEAD_DIM, 2 * d
    assert w_main.shape == (d, att_w + dn_w + z_w + gate_w) and t % tm == 0
    row = lambda w: pl.BlockSpec((tm, w), lambda i: (i, 0))
    const = lambda shp: pl.BlockSpec(shp, lambda i: (0, 0), pipeline_mode=pl.Buffered(1))
    est = (2 * (tm * d * 4 + tm * w_main.shape[1] * 2 + tm * LANES * 4) + w_main.size * 2 + w_small.size * 2
           + 3 * tm * d * 4 + 2 * tm * 512 * 4)
    return pl.pallas_call(
        functools.partial(_inproj_kernel, col_chunk=512),
        grid=(t // tm,),
        in_specs=[row(d), const((1, d)), const(w_main.shape), const(w_small.shape)],
        out_specs=[row(att_w), row(dn_w), row(z_w), row(gate_w), pl.BlockSpec((w_small.shape[0], tm), lambda i: (0, i))],
        out_shape=[jax.ShapeDtypeStruct((t, att_w), BF16), jax.ShapeDtypeStruct((t, dn_w), BF16),
                   jax.ShapeDtypeStruct((t, z_w), BF16), jax.ShapeDtypeStruct((t, gate_w), BF16),
                   jax.ShapeDtypeStruct((w_small.shape[0], t), F32)],
        compiler_params=pltpu.CompilerParams(dimension_semantics=("arbitrary",), vmem_limit_bytes=_vmem_limit(est)),
        name="inproj",
    )(x2, gain, w_main, w_small)


def _moba_kernel(q_ref, k_ref, v_ref, o_ref, *, nb):
    blk = MOBA_BLOCK
    lane = lax.broadcasted_iota(jnp.int32, (blk, LANES), 1)
    first_head = lane < HEAD_DIM

    km = jnp.concatenate(
        [jnp.sum(k_ref[0, n * blk:(n + 1) * blk, :].astype(F32), axis=0, keepdims=True) for n in range(nb)]
        + [jnp.zeros((8 - nb, LANES), F32)] * (1 if nb < 8 else 0), axis=0) * (1.0 / blk)
    l8 = lax.broadcasted_iota(jnp.int32, (8, LANES), 1) < HEAD_DIM
    pad = jnp.zeros((HEAD_DIM - 8, LANES), F32)
    g = jnp.concatenate([jnp.where(l8, 0.0, km), pad, jnp.where(l8, km, 0.0), pad], axis=0)
    g_hi = g.astype(BF16)
    g_lo = (g - g_hi.astype(F32)).astype(BF16)

    s_len = nb * blk
    kblk = lax.broadcasted_iota(jnp.int32, (s_len, LANES), 0) // blk
    klane = lax.broadcasted_iota(jnp.int32, (s_len, LANES), 1)
    k2 = k_ref[0]
    k_aug = (jnp.where(klane < HEAD_DIM, k2, jnp.where(klane - HEAD_DIM == kblk, 1.0, 0.0).astype(BF16)),
             jnp.where(klane < HEAD_DIM, jnp.where(klane == kblk, 1.0, 0.0).astype(BF16), k2))

    row = lax.broadcasted_iota(jnp.int32, (8, blk), 0)

    def drop_bias(s, j):
        out = jnp.zeros((8, blk), F32)
        past = row < j
        for n in range(j):
            rn = s[n:n + 1, :]
            beats = jnp.where(s > rn, 1.0, jnp.where(jnp.logical_and(s == rn, row < n), 1.0, 0.0))
            cnt = jnp.sum(jnp.where(past, beats, 0.0), axis=0, keepdims=True)
            out = jnp.where(row == n, jnp.where(cnt < float(MOBA_TOPK), 0.0, MASK_BIAS), out)
        return out

    def queries(j):
        q2 = q_ref[0, j * blk:(j + 1) * blk, :]
        qs = q2
        zero = jnp.zeros_like(qs)
        if j <= MOBA_TOPK:
            return jnp.where(first_head, qs, zero), jnp.where(first_head, zero, qs)
        s_t = (lax.dot_general(g_hi, q2, _NT, preferred_element_type=F32)
               + lax.dot_general(g_lo, q2, _NT, preferred_element_type=F32))
        padq = jnp.zeros((HEAD_DIM - 8, blk), F32)
        bias_t = jnp.concatenate([drop_bias(s_t[0:8], j), padq, drop_bias(s_t[HEAD_DIM:HEAD_DIM + 8], j), padq], axis=0)
        bias = bias_t.T.astype(BF16)
        return jnp.where(first_head, qs, bias), jnp.where(first_head, bias, qs)

    def scores(j):
        qa, qb = queries(j)
        width = (j + 1) * blk
        return (lax.dot_general(qa, k_aug[0][0:width], _NT, preferred_element_type=F32),
                lax.dot_general(qb, k_aug[1][0:width], _NT, preferred_element_type=F32))

    tri = lax.broadcasted_iota(jnp.int32, (blk, blk), 1) <= lax.broadcasted_iota(jnp.int32, (blk, blk), 0)

    v2 = v_ref[0]
    ones = jnp.ones_like(v2)
    v_aug = (jnp.where(klane < HEAD_DIM, v2, ones), jnp.where(klane < HEAD_DIM, ones, v2))
    sc = scores(0)
    for j in range(nb):
        nxt = scores(j + 1) if j + 1 < nb else None
        width = (j + 1) * blk
        outs = []
        for hd in range(2):
            parts = [sc[hd][:, n * blk:(n + 1) * blk] for n in range(j)]
            parts.append(jnp.where(tri, sc[hd][:, j * blk:width], NEG_BIG))
            m = jnp.max(parts[0], axis=1, keepdims=True)
            for pt in parts[1:]:
                m = jnp.maximum(m, jnp.max(pt, axis=1, keepdims=True))
            probs = jnp.concatenate([jnp.exp2(pt - m).astype(BF16) for pt in parts], axis=1)
            pv = _dot(probs, v_aug[hd][0:width])
            outs.append(pv / pltpu.roll(pv, HEAD_DIM, axis=1))
        o_ref[0, j * blk:(j + 1) * blk, :] = jnp.where(first_head, outs[0], outs[1]).astype(o_ref.dtype)
        sc = nxt


def _moba(qkv):
    b, s, w3 = qkv.shape
    blk = MOBA_BLOCK
    npair = ATT_HEADS // 2
    assert w3 == 3 * ATT_HEADS * HEAD_DIM and s % blk == 0 and s // blk <= 8
    nb = s // blk
    est = 2 * 4 * s * LANES * 2 + 4 * blk * s * (4 + 4 + 2)
    return pl.pallas_call(
        functools.partial(_moba_kernel, nb=nb),
        grid=(b, npair),
        in_specs=[pl.BlockSpec((1, s, LANES), lambda i, p: (i, 0, p)),
                  pl.BlockSpec((1, s, LANES), lambda i, p: (i, 0, npair + p)),
                  pl.BlockSpec((1, s, LANES), lambda i, p: (i, 0, 2 * npair + p))],
        out_specs=pl.BlockSpec((1, s, LANES), lambda i, p: (i, 0, p)),
        out_shape=jax.ShapeDtypeStruct((b, s, ATT_HEADS * HEAD_DIM), BF16),
        compiler_params=pltpu.CompilerParams(dimension_semantics=("arbitrary", "arbitrary"),
                                             vmem_limit_bytes=_vmem_limit(est)),
        name="moba",
    )(qkv, qkv, qkv)


_STG_KB, _STG_QN, _STG_KN, _STG_VB, _STG_KBE, _STG_QD = range(6)


def _gdn_kernel(x_ref, sm_ref, cw_ref, alog_ref, dtb_ref, utri_ref, exp_ref, hsum_ref, o_ref,
                xs_ref, st_ref, stg_ref, gx_ref, *, rows, groups_per_seq):
    ck = DN_CHUNK
    dn_w = DN_HEADS * HEAD_DIM
    npair = DN_HEADS // 2
    t = pl.program_id(0)
    halo = 8

    @pl.when(t == 0)
    def _():
        xs_ref[0:halo, :] = jnp.zeros((halo, 3 * dn_w), F32)
        st_ref[...] = jnp.zeros_like(st_ref)
        stg_ref[...] = jnp.zeros_like(stg_ref)
        gx_ref[...] = jnp.zeros_like(gx_ref)

    ri = lax.broadcasted_iota(jnp.int32, (ck, dn_w), 0)
    ci = jnp.bitwise_and(lax.broadcasted_iota(jnp.int32, (ck, dn_w), 1), ck - 1)
    tri, strict, diag = ri >= ci, ri > ci, ri == ci
    r2 = lax.broadcasted_iota(jnp.int32, (LANES, LANES), 0) // HEAD_DIM
    c2 = lax.broadcasted_iota(jnp.int32, (LANES, LANES), 1) // HEAD_DIM
    bmask = r2 == c2
    eye2 = jnp.where(diag[:, :LANES], 1.0, 0.0)

    def bd2(a):
        a = a.astype(BF16)
        return jnp.where(bmask, jnp.concatenate([a, a], axis=0), jnp.zeros((LANES, LANES), BF16))

    nchunk = rows // ck
    chains = [(cc, p) for cc in range(nchunk) for p in range(npair)]
    rs = lambda cc: slice(cc * ck, (cc + 1) * ck)
    ls = lambda p: slice(p * LANES, (p + 1) * LANES)
    staged = lambda which, cc, p: stg_ref[which, rs(cc), ls(p)]

    gch = [gx_ref[rs(cc), :] for cc in range(nchunk)]
    glast = [g_[ck - 1:ck, :] for g_ in gch]
    dec, kdec, gl_exp = [], [], []
    for cc, g_ in enumerate(gch):
        grow = jnp.sum(jnp.where(diag, g_, 0.0), axis=0, keepdims=True)
        dec.append(jnp.where(tri, jnp.exp(jnp.where(tri, g_ - grow, 0.0)), 0.0))
        kdec.append((stg_ref[_STG_KN, rs(cc), :].astype(F32) * jnp.exp(glast[cc] - g_)).astype(BF16))
        gl_exp.append(jnp.exp(glast[cc]))
    qd_old = [stg_ref[_STG_QD, rs(cc), :] for cc in range(nchunk)]

    seq_start = lax.rem(t, groups_per_seq) == 0
    cw = cw_ref[...]
    hsum2 = hsum_ref[0:LANES, 0:LANES]
    act, sumsq = {}, {}

    def prepare(slab):
        cs = ls(slab)
        xs_ref[0:halo, cs] = jnp.where(seq_start, 0.0, xs_ref[0:halo, cs])
        xs_ref[halo:halo + rows, cs] = x_ref[0, :, cs].astype(F32)
        xf = xs_ref[:, cs]
        y = xf * cw[CONV_WIDTH - 1:CONV_WIDTH, cs]
        for j in range(1, CONV_WIDTH):
            y = y + pltpu.roll(xf, j, axis=0) * cw[CONV_WIDTH - 1 - j:CONV_WIDTH - j, cs]
        y = y[halo:]
        xs_ref[0:halo, cs] = xf[rows:]
        y = y * _sigmoid(y)
        act[slab] = y
        if slab < 2 * npair:
            sumsq[slab] = _dot((y * y).astype(BF16), hsum2)

    def stage(p):
        qn = act[p] * (lax.rsqrt(sumsq[p] + NORM_EPS) * (HEAD_DIM ** -0.5))
        kn = act[npair + p] * lax.rsqrt(sumsq[npair + p] + NORM_EPS)
        bx, eg = bexp[:, ls(p)], jnp.exp(gexp[:, ls(p)])
        kb = kn * bx
        stg_ref[_STG_KB, :, ls(p)] = kb.astype(BF16)
        stg_ref[_STG_QN, :, ls(p)] = qn.astype(BF16)
        stg_ref[_STG_KN, :, ls(p)] = kn.astype(BF16)
        stg_ref[_STG_VB, :, ls(p)] = (act[2 * npair + p] * bx).astype(BF16)
        stg_ref[_STG_KBE, :, ls(p)] = (kb * eg).astype(BF16)
        stg_ref[_STG_QD, :, ls(p)] = (qn * eg).astype(BF16)

    todo = list(range(3 * npair))

    def prepare_some(n):
        for _ in range(n):
            if todo:
                prepare(todo.pop(0))

    kq = {(cc, p): lax.dot_general(jnp.concatenate([staged(_STG_KB, cc, p), staged(_STG_QN, cc, p)], axis=0),
                                   bd2(staged(_STG_KN, cc, p)), _NT, preferred_element_type=F32)
          for cc, p in chains}
    vbd = {(cc, p): jnp.concatenate([bd2(staged(_STG_VB, cc, p)), bd2(staged(_STG_KBE, cc, p))], axis=1)
           for cc, p in chains}
    sm = sm_ref[...]
    widen = lambda a: jnp.concatenate([a] * (rows // LANES), axis=1)
    xa = sm + widen(dtb_ref[...])
    softplus = jnp.maximum(xa, 0.0) + jnp.log(1.0 + jnp.exp(-jnp.abs(xa)))
    g = -jnp.exp(widen(alog_ref[...])) * softplus
    utri = utri_ref[...]
    gc = sum(_dot(t_, utri) for t_ in _split3(g))
    prepare_some(2)
    neg_l = {(cc, p): -jnp.where(strict[:, ls(p)], kq[cc, p][:ck] * dec[cc][:, ls(p)], 0.0) for cc, p in chains}
    qk = {(cc, p): jnp.where(tri[:, ls(p)], kq[cc, p][ck:] * dec[cc][:, ls(p)], 0.0) for cc, p in chains}
    ssum = {ch: eye2 + neg_l[ch] for ch in chains}
    pw = {ch: _dot(neg_l[ch].astype(BF16), bd2(neg_l[ch])) for ch in chains}
    gate_row = lax.broadcasted_iota(jnp.int32, (2 * DN_HEADS, rows), 0)
    comb = jnp.where(gate_row < DN_HEADS, _sigmoid(sm), gc)
    ex = lax.dot_general(jnp.concatenate(_split3(comb), axis=0), exp_ref[...], _TN, preferred_element_type=F32)
    bexp, gexp = ex[:, :dn_w], ex[:, dn_w:]
    prepare_some(2)
    span = 2
    while span * 2 < ck:
        both = {ch: _dot(jnp.concatenate([ssum[ch], pw[ch]], axis=0).astype(BF16), bd2(pw[ch])) for ch in chains}
        prepare_some(2)
        ssum = {ch: ssum[ch] + both[ch][:ck] for ch in chains}
        pw = {ch: both[ch][ck:] for ch in chains}
        span *= 2
    corr = {ch: _dot(ssum[ch].astype(BF16), bd2(pw[ch])) for ch in chains}
    prepare_some(len(todo))
    uw = {ch: _dot((ssum[ch] + corr[ch]).astype(BF16), vbd[ch]) for ch in chains}

    fresh = lax.rem(jnp.maximum(t - 1, 0), groups_per_seq) == 0
    state = [jnp.where(fresh, 0.0, st_ref[p]) for p in range(npair)]
    for cc in range(nchunk):
        wq = [_dot(jnp.concatenate([uw[cc, p][:, LANES:].astype(BF16), qd_old[cc][:, ls(p)]], axis=0),
                   state[p].astype(BF16)) for p in range(npair)]
        if cc < npair:
            stage(cc)
        vnew = [uw[cc, p][:, :LANES] - wq[p][:ck] for p in range(npair)]
        intra = [_dot(qk[cc, p].astype(BF16), bd2(vnew[p])) for p in range(npair)]
        upd = [lax.dot_general(kdec[cc][:, ls(p)], vnew[p].astype(BF16), _TN, preferred_element_type=F32)
               for p in range(npair)]
        for p in range(npair):
            o_ref[0, rs(cc), ls(p)] = (wq[p][ck:] + intra[p]).astype(o_ref.dtype)
        state = [state[p] * gl_exp[cc][:, ls(p)] + jnp.where(bmask, upd[p], 0.0) for p in range(npair)]
    for p in range(npair):
        st_ref[p] = state[p]
    for p in range(nchunk, npair):
        stage(p)
    gx_ref[...] = gexp


def _gdn(qkv_dn, small, conv_w, a_log, dt_bias, *, rows):
    b, s, w3 = qkv_dn.shape
    dn_w = DN_HEADS * HEAD_DIM
    assert w3 == 3 * dn_w and s % rows == 0 and rows % DN_CHUNK == 0
    h = DN_HEADS
    nc = s // rows
    nsteps = b * nc
    assert small.shape == (2 * h, b * s) and rows % LANES == 0
    zeros = jnp.zeros((h,), F32)
    alog_col = jnp.tile(jnp.concatenate([zeros, a_log.astype(F32)])[:, None], (1, LANES))
    dtb_col = jnp.tile(jnp.concatenate([zeros, dt_bias.astype(F32)])[:, None], (1, LANES))
    r = jnp.arange(rows)
    utri = ((r[:, None] // DN_CHUNK == r[None, :] // DN_CHUNK) & (r[:, None] <= r[None, :])).astype(BF16)
    src = jnp.arange(2 * h)[:, None]
    dst = jnp.arange(2 * dn_w)[None, :]
    expander = ((dst // dn_w == src // h) & ((dst % dn_w) // HEAD_DIM == src % h)).astype(BF16)
    expander = jnp.tile(expander, (3, 1))
    hl = jnp.arange(dn_w) // HEAD_DIM
    hsum = (hl[:, None] == hl[None, :]).astype(BF16)
    const = lambda a: pl.BlockSpec(a.shape, lambda t: (0,) * a.ndim)

    def produced(t):
        g = jnp.minimum(t, nsteps - 1)
        return g // nc, g % nc, 0

    def consumed(t):
        g = jnp.maximum(t - 1, 0)
        return g // nc, g % nc, 0

    est = (2 * (rows * w3 * 2 + 2 * h * rows * 4 + rows * dn_w * 4) + (rows + 8) * w3 * 4 + 12 * rows * w3 * 4
           + 6 * rows * dn_w * 2 + rows * dn_w * 4 + 2 * (utri.size + expander.size + hsum.size) * 2)
    return pl.pallas_call(
        functools.partial(_gdn_kernel, rows=rows, groups_per_seq=nc),
        grid=(nsteps + 1,),
        in_specs=[pl.BlockSpec((1, rows, w3), produced),
                  pl.BlockSpec((2 * h, rows), lambda t: (0, jnp.minimum(t, nsteps - 1))),
                  const(conv_w), const(alog_col), const(dtb_col), const(utri), const(expander), const(hsum)],
        out_specs=pl.BlockSpec((1, rows, dn_w), consumed),
        out_shape=jax.ShapeDtypeStruct((b, s, dn_w), F32),
        scratch_shapes=[pltpu.VMEM((rows + 8, w3), F32), pltpu.VMEM((DN_HEADS // 2, LANES, LANES), F32),
                        pltpu.VMEM((6, rows, dn_w), BF16), pltpu.VMEM((rows, dn_w), F32)],
        compiler_params=pltpu.CompilerParams(dimension_semantics=("arbitrary",),
                                             vmem_limit_bytes=_vmem_limit(est)),
        name="gdn",
    )(qkv_dn, small, conv_w.astype(F32), alog_col, dtb_col, utri, expander, hsum)


def _mixout_kernel(x_ref, ya_ref, od_ref, z_ref, gate_ref, dnw_ref, hmean_ref, wa_ref, wd_ref, wo_ref, pn_ref, o_ref):
    d = x_ref.shape[1]
    od = od_ref[...]
    ms = _dot((od * od).astype(BF16), hmean_ref[...])
    z = z_ref[...].astype(F32)
    y_dn = od * lax.rsqrt(ms + NORM_EPS) * dnw_ref[...] * (z * _sigmoid(z))
    ga = _sigmoid(gate_ref[:, :d].astype(F32))
    gd = _sigmoid(gate_ref[:, d:].astype(F32))
    merged = ga * _dot(ya_ref[...], wa_ref[...]) + gd * _dot(y_dn.astype(BF16), wd_ref[...])
    y = _dot(merged.astype(BF16), wo_ref[...])
    o_ref[...] = x_ref[...] + _rms(y, pn_ref[...])


def _mixout(x2, y_att, o_dn, z, gates, dn_norm, wa, wd, wo, post_norm, *, tm):
    t, d = x2.shape
    dn_w = DN_HEADS * HEAD_DIM
    dnw_row = jnp.tile(dn_norm.astype(F32), DN_HEADS)[None, :]
    hl = jnp.arange(dn_w) // HEAD_DIM
    hmean = ((hl[:, None] == hl[None, :]).astype(F32) / HEAD_DIM).astype(BF16)
    row = lambda w: pl.BlockSpec((tm, w), lambda i: (i, 0))
    const = lambda a: pl.BlockSpec(a.shape, lambda i: (0, 0), pipeline_mode=pl.Buffered(1))
    est = (2 * (2 * tm * d * 4 + tm * dn_w * (2 + 4 + 2) + tm * 2 * d * 2)
           + (hmean.size + wa.size + wd.size + wo.size) * 2 + 8 * tm * d * 4)
    return pl.pallas_call(
        _mixout_kernel,
        grid=(t // tm,),
        in_specs=[row(d), row(dn_w), row(dn_w), row(dn_w), row(2 * d), const(dnw_row), const(hmean),
                  const(wa), const(wd), const(wo), const(post_norm)],
        out_specs=row(d),
        out_shape=jax.ShapeDtypeStruct((t, d), F32),
        compiler_params=pltpu.CompilerParams(dimension_semantics=("arbitrary",), vmem_limit_bytes=_vmem_limit(est)),
        name="mixout",
    )(x2, y_att, o_dn, z, gates, dnw_row, hmean, wa, wd, wo, post_norm)


def _mlp_kernel(x_ref, pre_ref, w1_ref, w2_ref, post_ref, o_ref, *, ff_chunk):
    x = x_ref[...]
    hb = _rms(x, pre_ref[...]).astype(BF16)
    acc = jnp.zeros(x.shape, F32)
    for c in range(0, w1_ref.shape[1], ff_chunk):
        a = jnp.maximum(_dot(hb, w1_ref[:, c:c + ff_chunk]), 0.0)
        acc = acc + _dot((a * a).astype(BF16), w2_ref[c:c + ff_chunk, :])
    o_ref[...] = x + _rms(acc, post_ref[...])


def _mlp(x1, pre, w1, w2, post, *, tm):
    t, d = x1.shape
    row = pl.BlockSpec((tm, d), lambda i: (i, 0))
    const = lambda a: pl.BlockSpec(a.shape, lambda i: (0, 0), pipeline_mode=pl.Buffered(1))
    est = 2 * (2 * tm * d * 4) + (w1.size + w2.size) * 2 + 6 * tm * d * 4 + 2 * tm * 1024 * 4
    return pl.pallas_call(
        functools.partial(_mlp_kernel, ff_chunk=1024),
        grid=(t // tm,),
        in_specs=[row, const(pre), const(w1), const(w2), const(post)],
        out_specs=row,
        out_shape=jax.ShapeDtypeStruct((t, d), F32),
        compiler_params=pltpu.CompilerParams(dimension_semantics=("arbitrary",), vmem_limit_bytes=_vmem_limit(est)),
        name="mlp",
    )(x1, pre, w1, w2, post)


def kernel(x, pre_norm_mix, w_in, conv_w, a_log, dt_bias, dn_norm, w_branch_att, w_branch_dn,
           w_out, post_norm_mix, pre_norm_mlp, w_mlp_in, w_mlp_out, post_norm_mlp):
    b, s, d = x.shape
    att_w = ATT_HEADS * HEAD_DIM
    dn_w = DN_HEADS * HEAD_DIM
    n_main = 3 * att_w + 3 * dn_w + dn_w
    tm = 1024
    x2 = x.reshape(b * s, d)
    for l in range(w_in.shape[0]):
        wl = w_in[l]
        q_scale = HEAD_DIM ** -0.5 * LOG2_E
        w_main = jnp.concatenate([wl[:, :att_w] * q_scale, wl[:, att_w:n_main], wl[:, n_main + 2 * DN_HEADS:]],
                                 axis=1).astype(BF16)
        w_small = wl[:, n_main:n_main + 2 * DN_HEADS].T.astype(BF16)
        qkv_att, qkv_dn, z, gates, small = _inproj(x2, pre_norm_mix[l][None, :], w_main, w_small, tm=tm)
        y_att = _moba(qkv_att.reshape(b, s, 3 * att_w))
        o_dn = _gdn(qkv_dn.reshape(b, s, 3 * dn_w), small, conv_w[l], a_log[l], dt_bias[l], rows=256)
        x2 = _mixout(x2, y_att.reshape(b * s, att_w), o_dn.reshape(b * s, dn_w), z, gates, dn_norm[l],
                     w_branch_att[l].astype(BF16), w_branch_dn[l].astype(BF16), w_out[l].astype(BF16),
                     post_norm_mix[l][None, :], tm=tm)
        x2 = _mlp(x2, pre_norm_mlp[l][None, :], w_mlp_in[l].astype(BF16), w_mlp_out[l].astype(BF16),
                  post_norm_mlp[l][None, :], tm=tm)
    return x2.reshape(b, s, d)
```

```python
import functools

import jax
import jax.numpy as jnp
from jax import lax
from jax.experimental import pallas as pl
from jax.experimental.pallas import tpu as pltpu

ATT_HEADS = 8
DN_HEADS = 8
HEAD_DIM = 64
MOBA_BLOCK = 256
MOBA_TOPK = 3
DN_CHUNK = 64
CONV_WIDTH = 4
NORM_EPS = 1e-6

LANES = 128
V7X_VMEM_BYTES = 64 * 1024 * 1024
VMEM_CAP_BYTES = V7X_VMEM_BYTES - 8 * 1024 * 1024

F32 = jnp.float32
BF16 = jnp.bfloat16
NEG_BIG = -1e30
MASK_BIAS = -(2.0 ** 100)
LOG2_E = 1.4426950408889634

_NT = (((1,), (1,)), ((), ()))
_TN = (((0,), (0,)), ((), ()))


def _vmem_limit(nbytes):
    return int(min(VMEM_CAP_BYTES, nbytes * 5 // 4 + (4 << 20)))


def _dot(a, b):
    return jnp.dot(a, b, preferred_element_type=F32)


def _rms(x, w):
    return x * lax.rsqrt(jnp.mean(x * x, axis=-1, keepdims=True) + NORM_EPS) * w


def _sigmoid(x):
    return 1.0 / (1.0 + jnp.exp(-x))


def _split3(x):
    hi = x.astype(BF16)
    r = x - hi.astype(F32)
    mid = r.astype(BF16)
    lo = (r - mid.astype(F32)).astype(BF16)
    return hi, mid, lo


def _inproj_kernel(x_ref, g_ref, wm_ref, ws_ref, att_ref, dn_ref, z_ref, gate_ref, small_ref, *, col_chunk):
    hb = _rms(x_ref[...], g_ref[...]).astype(BF16)
    start = 0
    for ref in (att_ref, dn_ref, z_ref, gate_ref):
        width = ref.shape[1]
        for c in range(0, width, col_chunk):
            ref[:, c:c + col_chunk] = _dot(hb, wm_ref[:, start + c:start + c + col_chunk]).astype(ref.dtype)
        start += width
    small_ref[...] = lax.dot_general(ws_ref[...], hb, _NT, preferred_element_type=F32)


def _inproj(x2, gain, w_main, w_small, *, tm):
    t, d = x2.shape
    att_w, dn_w, z_w, gate_w = 3 * ATT_HEADS * HEAD_DIM, 3 * DN_HEADS * HEAD_DIM, DN_HEADS * HEAD_DIM, 2 * d
    assert w_main.shape == (d, att_w + dn_w + z_w + gate_w) and t % tm == 0
    row = lambda w: pl.BlockSpec((tm, w), lambda i: (i, 0))
    const = lambda shp: pl.BlockSpec(shp, lambda i: (0, 0), pipeline_mode=pl.Buffered(1))
    est = (2 * (tm * d * 4 + tm * w_main.shape[1] * 2 + tm * LANES * 4) + w_main.size * 2 + w_small.size * 2
           + 3 * tm * d * 4 + 2 * tm * 512 * 4)
    return pl.pallas_call(
        functools.partial(_inproj_kernel, col_chunk=512),
        grid=(t // tm,),
        in_specs=[row(d), const((1, d)), const(w_main.shape), const(w_small.shape)],
        out_specs=[row(att_w), row(dn_w), row(z_w), row(gate_w), pl.BlockSpec((w_small.shape[0], tm), lambda i: (0, i))],
        out_shape=[jax.ShapeDtypeStruct((t, att_w), BF16), jax.ShapeDtypeStruct((t, dn_w), BF16),
                   jax.ShapeDtypeStruct((t, z_w), BF16), jax.ShapeDtypeStruct((t, gate_w), BF16),
                   jax.ShapeDtypeStruct((w_small.shape[0], t), F32)],
        compiler_params=pltpu.CompilerParams(dimension_semantics=("arbitrary",), vmem_limit_bytes=_vmem_limit(est)),
        name="inproj",
    )(x2, gain, w_main, w_small)


def _moba_kernel(q_ref, k_ref, v_ref, o_ref, *, nb):
    blk = MOBA_BLOCK
    lane = lax.broadcasted_iota(jnp.int32, (blk, LANES), 1)
    first_head = lane < HEAD_DIM

    km = jnp.concatenate(
        [jnp.sum(k_ref[0, n * blk:(n + 1) * blk, :].astype(F32), axis=0, keepdims=True) for n in range(nb)]
        + [jnp.zeros((8 - nb, LANES), F32)] * (1 if nb < 8 else 0), axis=0) * (1.0 / blk)
    l8 = lax.broadcasted_iota(jnp.int32, (8, LANES), 1) < HEAD_DIM
    pad = jnp.zeros((HEAD_DIM - 8, LANES), F32)
    g = jnp.concatenate([jnp.where(l8, 0.0, km), pad, jnp.where(l8, km, 0.0), pad], axis=0)
    g_hi = g.astype(BF16)
    g_lo = (g - g_hi.astype(F32)).astype(BF16)

    s_len = nb * blk
    kblk = lax.broadcasted_iota(jnp.int32, (s_len, LANES), 0) // blk
    klane = lax.broadcasted_iota(jnp.int32, (s_len, LANES), 1)
    k2 = k_ref[0]
    k_aug = (jnp.where(klane < HEAD_DIM, k2, jnp.where(klane - HEAD_DIM == kblk, 1.0, 0.0).astype(BF16)),
             jnp.where(klane < HEAD_DIM, jnp.where(klane == kblk, 1.0, 0.0).astype(BF16), k2))

    row = lax.broadcasted_iota(jnp.int32, (8, blk), 0)

    def drop_bias(s, j):
        out = jnp.zeros((8, blk), F32)
        past = row < j
        for n in range(j):
            rn = s[n:n + 1, :]
            beats = jnp.where(s > rn, 1.0, jnp.where(jnp.logical_and(s == rn, row < n), 1.0, 0.0))
            cnt = jnp.sum(jnp.where(past, beats, 0.0), axis=0, keepdims=True)
            out = jnp.where(row == n, jnp.where(cnt < float(MOBA_TOPK), 0.0, MASK_BIAS), out)
        return out

    def queries(j):
        q2 = q_ref[0, j * blk:(j + 1) * blk, :]
        qs = q2
        zero = jnp.zeros_like(qs)
        if j <= MOBA_TOPK:
            return jnp.where(first_head, qs, zero), jnp.where(first_head, zero, qs)
        s_t = (lax.dot_general(g_hi, q2, _NT, preferred_element_type=F32)
               + lax.dot_general(g_lo, q2, _NT, preferred_element_type=F32))
        padq = jnp.zeros((HEAD_DIM - 8, blk), F32)
        bias_t = jnp.concatenate([drop_bias(s_t[0:8], j), padq, drop_bias(s_t[HEAD_DIM:HEAD_DIM + 8], j), padq], axis=0)
        bias = bias_t.T.astype(BF16)
        return jnp.where(first_head, qs, bias), jnp.where(first_head, bias, qs)

    def scores(j0, j1):
        q0 = queries(j0)
        q1 = queries(j1) if j1 is not None else None
        out = []
        for hd in range(2):
            keys = k_aug[hd][0:(j0 + 1) * blk]
            if j1 is None:
                out.append((lax.dot_general(q0[hd], keys, _NT, preferred_element_type=F32), None))
            else:
                out.append((lax.dot_general(jnp.concatenate([q0[hd], q1[hd]], axis=0), keys, _NT, preferred_element_type=F32),
                            lax.dot_general(q1[hd], k_aug[hd][j1 * blk:(j1 + 1) * blk], _NT, preferred_element_type=F32)))
        return out

    tri = lax.broadcasted_iota(jnp.int32, (blk, blk), 1) <= lax.broadcasted_iota(jnp.int32, (blk, blk), 0)

    def softmax_terms(parts):
        m = jnp.max(parts[0], axis=1, keepdims=True)
        for pt in parts[1:]:
            m = jnp.maximum(m, jnp.max(pt, axis=1, keepdims=True))
        return [jnp.exp2(pt - m).astype(BF16) for pt in parts]

    def normalised(pv):
        return pv / pltpu.roll(pv, HEAD_DIM, axis=1)

    v2 = v_ref[0]
    ones = jnp.ones_like(v2)
    v_aug = (jnp.where(klane < HEAD_DIM, v2, ones), jnp.where(klane < HEAD_DIM, ones, v2))
    groups = [(j0, j0 + 1 if j0 + 1 < nb else None) for j0 in range(0, nb, 2)]
    sc = scores(*groups[0])
    for gi, (j0, j1) in enumerate(groups):
        nxt = scores(*groups[gi + 1]) if gi + 1 < len(groups) else None
        w0 = (j0 + 1) * blk
        outs0, outs1 = [], []
        for hd in range(2):
            shared, own1 = sc[hd]
            pr0 = softmax_terms([shared[:blk, n * blk:(n + 1) * blk] for n in range(j0)]
                                + [jnp.where(tri, shared[:blk, j0 * blk:w0], NEG_BIG)])
            if j1 is None:
                outs0.append(normalised(_dot(jnp.concatenate(pr0, axis=1), v_aug[hd][0:w0])))
                continue
            pr1 = softmax_terms([shared[blk:, n * blk:(n + 1) * blk] for n in range(j0 + 1)]
                                + [jnp.where(tri, own1, NEG_BIG)])
            both = jnp.concatenate([jnp.concatenate(pr0, axis=1), jnp.concatenate(pr1[:-1], axis=1)], axis=0)
            pv = _dot(both, v_aug[hd][0:w0])
            outs0.append(normalised(pv[:blk]))
            outs1.append(normalised(pv[blk:] + _dot(pr1[-1], v_aug[hd][j1 * blk:(j1 + 1) * blk])))
        o_ref[0, j0 * blk:(j0 + 1) * blk, :] = jnp.where(first_head, outs0[0], outs0[1]).astype(o_ref.dtype)
        if j1 is not None:
            o_ref[0, j1 * blk:(j1 + 1) * blk, :] = jnp.where(first_head, outs1[0], outs1[1]).astype(o_ref.dtype)
        sc = nxt


def _moba(qkv):
    b, s, w3 = qkv.shape
    blk = MOBA_BLOCK
    npair = ATT_HEADS // 2
    assert w3 == 3 * ATT_HEADS * HEAD_DIM and s % blk == 0 and s // blk <= 8
    nb = s // blk
    est = 2 * 4 * s * LANES * 2 + 8 * blk * s * (4 + 4 + 2)
    return pl.pallas_call(
        functools.partial(_moba_kernel, nb=nb),
        grid=(b, npair),
        in_specs=[pl.BlockSpec((1, s, LANES), lambda i, p: (i, 0, p)),
                  pl.BlockSpec((1, s, LANES), lambda i, p: (i, 0, npair + p)),
                  pl.BlockSpec((1, s, LANES), lambda i, p: (i, 0, 2 * npair + p))],
        out_specs=pl.BlockSpec((1, s, LANES), lambda i, p: (i, 0, p)),
        out_shape=jax.ShapeDtypeStruct((b, s, ATT_HEADS * HEAD_DIM), BF16),
        compiler_params=pltpu.CompilerParams(dimension_semantics=("arbitrary", "arbitrary"),
                                             vmem_limit_bytes=_vmem_limit(est)),
        name="moba",
    )(qkv, qkv, qkv)


_STG_KB, _STG_QN, _STG_KN, _STG_VB, _STG_KBE, _STG_QD = range(6)


def _gdn_kernel(x_ref, sm_ref, cw_ref, alog_ref, dtb_ref, utri_ref, exp_ref, hsum_ref, o_ref,
                xs_ref, st_ref, stg_ref, gx_ref, *, rows, groups_per_seq):
    ck = DN_CHUNK
    dn_w = DN_HEADS * HEAD_DIM
    npair = DN_HEADS // 2
    t = pl.program_id(0)
    halo = 8

    @pl.when(t == 0)
    def _():
        xs_ref[0:halo, :] = jnp.zeros((halo, 3 * dn_w), F32)
        st_ref[...] = jnp.zeros_like(st_ref)
        stg_ref[...] = jnp.zeros_like(stg_ref)
        gx_ref[...] = jnp.zeros_like(gx_ref)

    ri = lax.broadcasted_iota(jnp.int32, (ck, dn_w), 0)
    ci = jnp.bitwise_and(lax.broadcasted_iota(jnp.int32, (ck, dn_w), 1), ck - 1)
    tri, strict, diag = ri >= ci, ri > ci, ri == ci
    r2 = lax.broadcasted_iota(jnp.int32, (LANES, LANES), 0) // HEAD_DIM
    c2 = lax.broadcasted_iota(jnp.int32, (LANES, LANES), 1) // HEAD_DIM
    bmask = r2 == c2
    eye2 = jnp.where(diag[:, :LANES], 1.0, 0.0)

    def bd2(a):
        a = a.astype(BF16)
        return jnp.where(bmask, jnp.concatenate([a, a], axis=0), jnp.zeros((LANES, LANES), BF16))

    nchunk = rows // ck
    chains = [(cc, p) for cc in range(nchunk) for p in range(npair)]
    rs = lambda cc: slice(cc * ck, (cc + 1) * ck)
    ls = lambda p: slice(p * LANES, (p + 1) * LANES)
    staged = lambda which, cc, p: stg_ref[which, rs(cc), ls(p)]

    gch = [gx_ref[rs(cc), :] for cc in range(nchunk)]
    glast = [g_[ck - 1:ck, :] for g_ in gch]
    dec, kdec, gl_exp = [], [], []
    for cc, g_ in enumerate(gch):
        grow = jnp.sum(jnp.where(diag, g_, 0.0), axis=0, keepdims=True)
        dec.append(jnp.where(tri, jnp.exp(jnp.where(tri, g_ - grow, 0.0)), 0.0))
        kdec.append((stg_ref[_STG_KN, rs(cc), :].astype(F32) * jnp.exp(glast[cc] - g_)).astype(BF16))
        gl_exp.append(jnp.exp(glast[cc]))
    qd_old = [stg_ref[_STG_QD, rs(cc), :] for cc in range(nchunk)]

    seq_start = lax.rem(t, groups_per_seq) == 0
    cw = cw_ref[...]
    hsum2 = hsum_ref[0:LANES, 0:LANES]
    act, sumsq = {}, {}

    def prepare(slab):
        cs = ls(slab)
        xs_ref[0:halo, cs] = jnp.where(seq_start, 0.0, xs_ref[0:halo, cs])
        xs_ref[halo:halo + rows, cs] = x_ref[0, :, cs].astype(F32)
        xf = xs_ref[:, cs]
        y = xf * cw[CONV_WIDTH - 1:CONV_WIDTH, cs]
        for j in range(1, CONV_WIDTH):
            y = y + pltpu.roll(xf, j, axis=0) * cw[CONV_WIDTH - 1 - j:CONV_WIDTH - j, cs]
        y = y[halo:]
        xs_ref[0:halo, cs] = xf[rows:]
        y = y * _sigmoid(y)
        act[slab] = y
        if slab < 2 * npair:
            sumsq[slab] = _dot((y * y).astype(BF16), hsum2)

    def stage(p):
        qn = act[p] * (lax.rsqrt(sumsq[p] + NORM_EPS) * (HEAD_DIM ** -0.5))
        kn = act[npair + p] * lax.rsqrt(sumsq[npair + p] + NORM_EPS)
        bx, eg = bexp[:, ls(p)], jnp.exp(gexp[:, ls(p)])
        kb = kn * bx
        stg_ref[_STG_KB, :, ls(p)] = kb.astype(BF16)
        stg_ref[_STG_QN, :, ls(p)] = qn.astype(BF16)
        stg_ref[_STG_KN, :, ls(p)] = kn.astype(BF16)
        stg_ref[_STG_VB, :, ls(p)] = (act[2 * npair + p] * bx).astype(BF16)
        stg_ref[_STG_KBE, :, ls(p)] = (kb * eg).astype(BF16)
        stg_ref[_STG_QD, :, ls(p)] = (qn * eg).astype(BF16)

    todo = list(range(3 * npair))

    def prepare_some(n):
        for _ in range(n):
            if todo:
                prepare(todo.pop(0))

    kq = {(cc, p): lax.dot_general(jnp.concatenate([staged(_STG_KB, cc, p), staged(_STG_QN, cc, p)], axis=0),
                                   bd2(staged(_STG_KN, cc, p)), _NT, preferred_element_type=F32)
          for cc, p in chains}
    vbd = {(cc, p): jnp.concatenate([bd2(staged(_STG_VB, cc, p)), bd2(staged(_STG_KBE, cc, p))], axis=1)
           for cc, p in chains}
    sm = sm_ref[...]
    widen = lambda a: jnp.concatenate([a] * (rows // LANES), axis=1)
    xa = sm + widen(dtb_ref[...])
    softplus = jnp.maximum(xa, 0.0) + jnp.log(1.0 + jnp.exp(-jnp.abs(xa)))
    g = -jnp.exp(widen(alog_ref[...])) * softplus
    utri = utri_ref[...]
    gc = sum(_dot(t_, utri) for t_ in _split3(g))
    prepare_some(2)
    neg_l = {(cc, p): -jnp.where(strict[:, ls(p)], kq[cc, p][:ck] * dec[cc][:, ls(p)], 0.0) for cc, p in chains}
    qk = {(cc, p): jnp.where(tri[:, ls(p)], kq[cc, p][ck:] * dec[cc][:, ls(p)], 0.0) for cc, p in chains}
    ssum = {ch: eye2 + neg_l[ch] for ch in chains}
    pw = {ch: _dot(neg_l[ch].astype(BF16), bd2(neg_l[ch])) for ch in chains}
    gate_row = lax.broadcasted_iota(jnp.int32, (2 * DN_HEADS, rows), 0)
    comb = jnp.where(gate_row < DN_HEADS, _sigmoid(sm), gc)
    ex = lax.dot_general(jnp.concatenate(_split3(comb), axis=0), exp_ref[...], _TN, preferred_element_type=F32)
    bexp, gexp = ex[:, :dn_w], ex[:, dn_w:]
    prepare_some(2)
    span = 2
    while span * 2 < ck:
        both = {ch: _dot(jnp.concatenate([ssum[ch], pw[ch]], axis=0).astype(BF16), bd2(pw[ch])) for ch in chains}
        prepare_some(2)
        ssum = {ch: ssum[ch] + both[ch][:ck] for ch in chains}
        pw = {ch: both[ch][ck:] for ch in chains}
        span *= 2
    corr = {ch: _dot(ssum[ch].astype(BF16), bd2(pw[ch])) for ch in chains}
    prepare_some(len(todo))
    uw = {ch: _dot((ssum[ch] + corr[ch]).astype(BF16), vbd[ch]) for ch in chains}

    fresh = lax.rem(jnp.maximum(t - 1, 0), groups_per_seq) == 0
    state = [jnp.where(fresh, 0.0, st_ref[p]) for p in range(npair)]
    for cc in range(nchunk):
        wq = [_dot(jnp.concatenate([uw[cc, p][:, LANES:].astype(BF16), qd_old[cc][:, ls(p)]], axis=0),
                   state[p].astype(BF16)) for p in range(npair)]
        if cc < npair:
            stage(cc)
        vnew = [uw[cc, p][:, :LANES] - wq[p][:ck] for p in range(npair)]
        intra = [_dot(qk[cc, p].astype(BF16), bd2(vnew[p])) for p in range(npair)]
        upd = [lax.dot_general(kdec[cc][:, ls(p)], vnew[p].astype(BF16), _TN, preferred_element_type=F32)
               for p in range(npair)]
        for p in range(npair):
            o_ref[0, rs(cc), ls(p)] = (wq[p][ck:] + intra[p]).astype(o_ref.dtype)
        state = [state[p] * gl_exp[cc][:, ls(p)] + jnp.where(bmask, upd[p], 0.0) for p in range(npair)]
    for p in range(npair):
        st_ref[p] = state[p]
    for p in range(nchunk, npair):
        stage(p)
    gx_ref[...] = gexp


def _gdn(qkv_dn, small, conv_w, a_log, dt_bias, *, rows):
    b, s, w3 = qkv_dn.shape
    dn_w = DN_HEADS * HEAD_DIM
    assert w3 == 3 * dn_w and s % rows == 0 and rows % DN_CHUNK == 0
    h = DN_HEADS
    nc = s // rows
    nsteps = b * nc
    assert small.shape == (2 * h, b * s) and rows % LANES == 0
    zeros = jnp.zeros((h,), F32)
    alog_col = jnp.tile(jnp.concatenate([zeros, a_log.astype(F32)])[:, None], (1, LANES))
    dtb_col = jnp.tile(jnp.concatenate([zeros, dt_bias.astype(F32)])[:, None], (1, LANES))
    r = jnp.arange(rows)
    utri = ((r[:, None] // DN_CHUNK == r[None, :] // DN_CHUNK) & (r[:, None] <= r[None, :])).astype(BF16)
    src = jnp.arange(2 * h)[:, None]
    dst = jnp.arange(2 * dn_w)[None, :]
    expander = ((dst // dn_w == src // h) & ((dst % dn_w) // HEAD_DIM == src % h)).astype(BF16)
    expander = jnp.tile(expander, (3, 1))
    hl = jnp.arange(dn_w) // HEAD_DIM
    hsum = (hl[:, None] == hl[None, :]).astype(BF16)
    const = lambda a: pl.BlockSpec(a.shape, lambda t: (0,) * a.ndim)

    def produced(t):
        g = jnp.minimum(t, nsteps - 1)
        return g // nc, g % nc, 0

    def consumed(t):
        g = jnp.maximum(t - 1, 0)
        return g // nc, g % nc, 0

    est = (2 * (rows * w3 * 2 + 2 * h * rows * 4 + rows * dn_w * 4) + (rows + 8) * w3 * 4 + 12 * rows * w3 * 4
           + 6 * rows * dn_w * 2 + rows * dn_w * 4 + 2 * (utri.size + expander.size + hsum.size) * 2)
    return pl.pallas_call(
        functools.partial(_gdn_kernel, rows=rows, groups_per_seq=nc),
        grid=(nsteps + 1,),
        in_specs=[pl.BlockSpec((1, rows, w3), produced),
                  pl.BlockSpec((2 * h, rows), lambda t: (0, jnp.minimum(t, nsteps - 1))),
                  const(conv_w), const(alog_col), const(dtb_col), const(utri), const(expander), const(hsum)],
        out_specs=pl.BlockSpec((1, rows, dn_w), consumed),
        out_shape=jax.ShapeDtypeStruct((b, s, dn_w), F32),
        scratch_shapes=[pltpu.VMEM((rows + 8, w3), F32), pltpu.VMEM((DN_HEADS // 2, LANES, LANES), F32),
                        pltpu.VMEM((6, rows, dn_w), BF16), pltpu.VMEM((rows, dn_w), F32)],
        compiler_params=pltpu.CompilerParams(dimension_semantics=("arbitrary",),
                                             vmem_limit_bytes=_vmem_limit(est)),
        name="gdn",
    )(qkv_dn, small, conv_w.astype(F32), alog_col, dtb_col, utri, expander, hsum)


def _mixout_kernel(x_ref, ya_ref, od_ref, z_ref, gate_ref, dnw_ref, hmean_ref, wa_ref, wd_ref, wo_ref, pn_ref, o_ref):
    d = x_ref.shape[1]
    od = od_ref[...]
    ms = _dot((od * od).astype(BF16), hmean_ref[...])
    z = z_ref[...].astype(F32)
    y_dn = od * lax.rsqrt(ms + NORM_EPS) * dnw_ref[...] * (z * _sigmoid(z))
    ga = _sigmoid(gate_ref[:, :d].astype(F32))
    gd = _sigmoid(gate_ref[:, d:].astype(F32))
    merged = ga * _dot(ya_ref[...], wa_ref[...]) + gd * _dot(y_dn.astype(BF16), wd_ref[...])
    y = _dot(merged.astype(BF16), wo_ref[...])
    o_ref[...] = x_ref[...] + _rms(y, pn_ref[...])


def _mixout(x2, y_att, o_dn, z, gates, dn_norm, wa, wd, wo, post_norm, *, tm):
    t, d = x2.shape
    dn_w = DN_HEADS * HEAD_DIM
    dnw_row = jnp.tile(dn_norm.astype(F32), DN_HEADS)[None, :]
    hl = jnp.arange(dn_w) // HEAD_DIM
    hmean = ((hl[:, None] == hl[None, :]).astype(F32) / HEAD_DIM).astype(BF16)
    row = lambda w: pl.BlockSpec((tm, w), lambda i: (i, 0))
    const = lambda a: pl.BlockSpec(a.shape, lambda i: (0, 0), pipeline_mode=pl.Buffered(1))
    est = (2 * (2 * tm * d * 4 + tm * dn_w * (2 + 4 + 2) + tm * 2 * d * 2)
           + (hmean.size + wa.size + wd.size + wo.size) * 2 + 8 * tm * d * 4)
    return pl.pallas_call(
        _mixout_kernel,
        grid=(t // tm,),
        in_specs=[row(d), row(dn_w), row(dn_w), row(dn_w), row(2 * d), const(dnw_row), const(hmean),
                  const(wa), const(wd), const(wo), const(post_norm)],
        out_specs=row(d),
        out_shape=jax.ShapeDtypeStruct((t, d), F32),
        compiler_params=pltpu.CompilerParams(dimension_semantics=("arbitrary",), vmem_limit_bytes=_vmem_limit(est)),
        name="mixout",
    )(x2, y_att, o_dn, z, gates, dnw_row, hmean, wa, wd, wo, post_norm)


def _mlp_kernel(x_ref, pre_ref, w1_ref, w2_ref, post_ref, o_ref, *, ff_chunk):
    x = x_ref[...]
    hb = _rms(x, pre_ref[...]).astype(BF16)
    acc = jnp.zeros(x.shape, F32)
    for c in range(0, w1_ref.shape[1], ff_chunk):
        a = jnp.maximum(_dot(hb, w1_ref[:, c:c + ff_chunk]), 0.0)
        acc = acc + _dot((a * a).astype(BF16), w2_ref[c:c + ff_chunk, :])
    o_ref[...] = x + _rms(acc, post_ref[...])


def _mlp(x1, pre, w1, w2, post, *, tm):
    t, d = x1.shape
    row = pl.BlockSpec((tm, d), lambda i: (i, 0))
    const = lambda a: pl.BlockSpec(a.shape, lambda i: (0, 0), pipeline_mode=pl.Buffered(1))
    est = 2 * (2 * tm * d * 4) + (w1.size + w2.size) * 2 + 6 * tm * d * 4 + 2 * tm * 1024 * 4
    return pl.pallas_call(
        functools.partial(_mlp_kernel, ff_chunk=1024),
        grid=(t // tm,),
        in_specs=[row, const(pre), const(w1), const(w2), const(post)],
        out_specs=row,
        out_shape=jax.ShapeDtypeStruct((t, d), F32),
        compiler_params=pltpu.CompilerParams(dimension_semantics=("arbitrary",), vmem_limit_bytes=_vmem_limit(est)),
        name="mlp",
    )(x1, pre, w1, w2, post)


def kernel(x, pre_norm_mix, w_in, conv_w, a_log, dt_bias, dn_norm, w_branch_att, w_branch_dn,
           w_out, post_norm_mix, pre_norm_mlp, w_mlp_in, w_mlp_out, post_norm_mlp):
    b, s, d = x.shape
    att_w = ATT_HEADS * HEAD_DIM
    dn_w = DN_HEADS * HEAD_DIM
    n_main = 3 * att_w + 3 * dn_w + dn_w
    tm = 1024
    x2 = x.reshape(b * s, d)
    for l in range(w_in.shape[0]):
        wl = w_in[l]
        q_scale = HEAD_DIM ** -0.5 * LOG2_E
        w_main = jnp.concatenate([wl[:, :att_w] * q_scale, wl[:, att_w:n_main], wl[:, n_main + 2 * DN_HEADS:]],
                                 axis=1).astype(BF16)
        w_small = wl[:, n_main:n_main + 2 * DN_HEADS].T.astype(BF16)
        qkv_att, qkv_dn, z, gates, small = _inproj(x2, pre_norm_mix[l][None, :], w_main, w_small, tm=tm)
        y_att = _moba(qkv_att.reshape(b, s, 3 * att_w))
        o_dn = _gdn(qkv_dn.reshape(b, s, 3 * dn_w), small, conv_w[l], a_log[l], dt_bias[l], rows=256)
        x2 = _mixout(x2, y_att.reshape(b * s, att_w), o_dn.reshape(b * s, dn_w), z, gates, dn_norm[l],
                     w_branch_att[l].astype(BF16), w_branch_dn[l].astype(BF16), w_out[l].astype(BF16),
                     post_norm_mix[l][None, :], tm=tm)
        x2 = _mlp(x2, pre_norm_mlp[l][None, :], w_mlp_in[l].astype(BF16), w_mlp_out[l].astype(BF16),
                  post_norm_mlp[l][None, :], tm=tm)
    return x2.reshape(b, s, d)
```

```python
import functools

import jax
import jax.numpy as jnp
from jax import lax
from jax.experimental import pallas as pl
from jax.experimental.pallas import tpu as pltpu

ATT_HEADS = 8
DN_HEADS = 8
HEAD_DIM = 64
MOBA_BLOCK = 256
MOBA_TOPK = 3
DN_CHUNK = 64
CONV_WIDTH = 4
NORM_EPS = 1e-6

LANES = 128
V7X_VMEM_BYTES = 64 * 1024 * 1024
VMEM_CAP_BYTES = V7X_VMEM_BYTES - 8 * 1024 * 1024

F32 = jnp.float32
BF16 = jnp.bfloat16
NEG_BIG = -1e30
MASK_BIAS = -(2.0 ** 100)
LOG2_E = 1.4426950408889634

_NT = (((1,), (1,)), ((), ()))
_TN = (((0,), (0,)), ((), ()))


def _vmem_limit(nbytes):
    return int(min(VMEM_CAP_BYTES, nbytes * 5 // 4 + (4 << 20)))


def _dot(a, b):
    return jnp.dot(a, b, preferred_element_type=F32)


def _rms(x, w):
    return x * lax.rsqrt(jnp.mean(x * x, axis=-1, keepdims=True) + NORM_EPS) * w


def _sigmoid(x):
    return 1.0 / (1.0 + jnp.exp(-x))


def _split3(x):
    hi = x.astype(BF16)
    r = x - hi.astype(F32)
    mid = r.astype(BF16)
    lo = (r - mid.astype(F32)).astype(BF16)
    return hi, mid, lo


def _inproj_kernel(x_ref, g_ref, wm_ref, ws_ref, att_ref, dn_ref, z_ref, gate_ref, small_ref, *, col_chunk):
    hb = _rms(x_ref[...], g_ref[...]).astype(BF16)
    start = 0
    for ref in (att_ref, dn_ref, z_ref, gate_ref):
        width = ref.shape[1]
        for c in range(0, width, col_chunk):
            ref[:, c:c + col_chunk] = _dot(hb, wm_ref[:, start + c:start + c + col_chunk]).astype(ref.dtype)
        start += width
    small_ref[...] = lax.dot_general(ws_ref[...], hb, _NT, preferred_element_type=F32)


def _inproj(x2, gain, w_main, w_small, *, tm):
    t, d = x2.shape
    att_w, dn_w, z_w, gate_w = 3 * ATT_HEADS * HEAD_DIM, 3 * DN_HEADS * HEAD_DIM, DN_HEADS * HEAD_DIM, 2 * d
    assert w_main.shape == (d, att_w + dn_w + z_w + gate_w) and t % tm == 0
    row = lambda w: pl.BlockSpec((tm, w), lambda i: (i, 0))
    const = lambda shp: pl.BlockSpec(shp, lambda i: (0, 0), pipeline_mode=pl.Buffered(1))
    est = (2 * (tm * d * 4 + tm * w_main.shape[1] * 2 + tm * LANES * 4) + w_main.size * 2 + w_small.size * 2
           + 3 * tm * d * 4 + 2 * tm * 512 * 4)
    return pl.pallas_call(
        functools.partial(_inproj_kernel, col_chunk=512),
        grid=(t // tm,),
        in_specs=[row(d), const((1, d)), const(w_main.shape), const(w_small.shape)],
        out_specs=[row(att_w), row(dn_w), row(z_w), row(gate_w), pl.BlockSpec((w_small.shape[0], tm), lambda i: (0, i))],
        out_shape=[jax.ShapeDtypeStruct((t, att_w), BF16), jax.ShapeDtypeStruct((t, dn_w), BF16),
                   jax.ShapeDtypeStruct((t, z_w), BF16), jax.ShapeDtypeStruct((t, gate_w), BF16),
                   jax.ShapeDtypeStruct((w_small.shape[0], t), F32)],
        compiler_params=pltpu.CompilerParams(dimension_semantics=("arbitrary",), vmem_limit_bytes=_vmem_limit(est)),
        name="inproj",
    )(x2, gain, w_main, w_small)


def _moba_kernel(q_ref, k_ref, v_ref, o_ref, *, nb):
    blk = MOBA_BLOCK
    lane = lax.broadcasted_iota(jnp.int32, (blk, LANES), 1)
    first_head = lane < HEAD_DIM

    km = jnp.concatenate(
        [jnp.sum(k_ref[0, n * blk:(n + 1) * blk, :].astype(F32), axis=0, keepdims=True) for n in range(nb)]
        + [jnp.zeros((8 - nb, LANES), F32)] * (1 if nb < 8 else 0), axis=0) * (1.0 / blk)
    l8 = lax.broadcasted_iota(jnp.int32, (8, LANES), 1) < HEAD_DIM
    pad = jnp.zeros((HEAD_DIM - 8, LANES), F32)
    g = jnp.concatenate([jnp.where(l8, 0.0, km), pad, jnp.where(l8, km, 0.0), pad], axis=0)
    g_hi = g.astype(BF16)
    g_lo = (g - g_hi.astype(F32)).astype(BF16)

    s_len = nb * blk
    kblk = lax.broadcasted_iota(jnp.int32, (s_len, LANES), 0) // blk
    klane = lax.broadcasted_iota(jnp.int32, (s_len, LANES), 1)
    k2 = k_ref[0]
    k_aug = (jnp.where(klane < HEAD_DIM, k2, jnp.where(klane - HEAD_DIM == kblk, 1.0, 0.0).astype(BF16)),
             jnp.where(klane < HEAD_DIM, jnp.where(klane == kblk, 1.0, 0.0).astype(BF16), k2))

    row = lax.broadcasted_iota(jnp.int32, (8, blk), 0)

    def drop_bias(s, j):
        out = jnp.zeros((8, blk), F32)
        past = row < j
        for n in range(j):
            rn = s[n:n + 1, :]
            beats = jnp.where(s > rn, 1.0, jnp.where(jnp.logical_and(s == rn, row < n), 1.0, 0.0))
            cnt = jnp.sum(jnp.where(past, beats, 0.0), axis=0, keepdims=True)
            out = jnp.where(row == n, jnp.where(cnt < float(MOBA_TOPK), 0.0, MASK_BIAS), out)
        return out

    def queries(j):
        q2 = q_ref[0, j * blk:(j + 1) * blk, :]
        qs = q2
        zero = jnp.zeros_like(qs)
        if j <= MOBA_TOPK:
            return jnp.where(first_head, qs, zero), jnp.where(first_head, zero, qs)
        s_t = (lax.dot_general(g_hi, q2, _NT, preferred_element_type=F32)
               + lax.dot_general(g_lo, q2, _NT, preferred_element_type=F32))
        padq = jnp.zeros((HEAD_DIM - 8, blk), F32)
        bias_t = jnp.concatenate([drop_bias(s_t[0:8], j), padq, drop_bias(s_t[HEAD_DIM:HEAD_DIM + 8], j), padq], axis=0)
        bias = bias_t.T.astype(BF16)
        return jnp.where(first_head, qs, bias), jnp.where(first_head, bias, qs)

    def scores(j0, j1):
        q0 = queries(j0)
        q1 = queries(j1) if j1 is not None else None
        out = []
        for hd in range(2):
            keys = k_aug[hd][0:(j0 + 1) * blk]
            if j1 is None:
                out.append((lax.dot_general(q0[hd], keys, _NT, preferred_element_type=F32), None))
            else:
                out.append((lax.dot_general(jnp.concatenate([q0[hd], q1[hd]], axis=0), keys, _NT, preferred_element_type=F32),
                            lax.dot_general(q1[hd], k_aug[hd][j1 * blk:(j1 + 1) * blk], _NT, preferred_element_type=F32)))
        return out

    tri = lax.broadcasted_iota(jnp.int32, (blk, blk), 1) <= lax.broadcasted_iota(jnp.int32, (blk, blk), 0)

    def softmax_terms(parts):
        m = jnp.max(parts[0], axis=1, keepdims=True)
        for pt in parts[1:]:
            m = jnp.maximum(m, jnp.max(pt, axis=1, keepdims=True))
        return [jnp.exp2(pt - m).astype(BF16) for pt in parts]

    def normalised(pv):
        return pv / pltpu.roll(pv, HEAD_DIM, axis=1)

    v2 = v_ref[0]
    ones = jnp.ones_like(v2)
    v_aug = (jnp.where(klane < HEAD_DIM, v2, ones), jnp.where(klane < HEAD_DIM, ones, v2))
    groups = [(j0, j0 + 1 if j0 + 1 < nb else None) for j0 in range(0, nb, 2)]
    sc = scores(*groups[0])
    for gi, (j0, j1) in enumerate(groups):
        nxt = scores(*groups[gi + 1]) if gi + 1 < len(groups) else None
        w0 = (j0 + 1) * blk
        outs0, outs1 = [], []
        for hd in range(2):
            shared, own1 = sc[hd]
            pr0 = softmax_terms([shared[:blk, n * blk:(n + 1) * blk] for n in range(j0)]
                                + [jnp.where(tri, shared[:blk, j0 * blk:w0], NEG_BIG)])
            if j1 is None:
                outs0.append(normalised(_dot(jnp.concatenate(pr0, axis=1), v_aug[hd][0:w0])))
                continue
            pr1 = softmax_terms([shared[blk:, n * blk:(n + 1) * blk] for n in range(j0 + 1)]
                                + [jnp.where(tri, own1, NEG_BIG)])
            both = jnp.concatenate([jnp.concatenate(pr0, axis=1), jnp.concatenate(pr1[:-1], axis=1)], axis=0)
            pv = _dot(both, v_aug[hd][0:w0])
            outs0.append(normalised(pv[:blk]))
            outs1.append(normalised(pv[blk:] + _dot(pr1[-1], v_aug[hd][j1 * blk:(j1 + 1) * blk])))
        o_ref[0, j0 * blk:(j0 + 1) * blk, :] = jnp.where(first_head, outs0[0], outs0[1]).astype(o_ref.dtype)
        if j1 is not None:
            o_ref[0, j1 * blk:(j1 + 1) * blk, :] = jnp.where(first_head, outs1[0], outs1[1]).astype(o_ref.dtype)
        sc = nxt


def _moba(qkv):
    b, s, w3 = qkv.shape
    blk = MOBA_BLOCK
    npair = ATT_HEADS // 2
    assert w3 == 3 * ATT_HEADS * HEAD_DIM and s % blk == 0 and s // blk <= 8
    nb = s // blk
    est = 2 * 4 * s * LANES * 2 + 8 * blk * s * (4 + 4 + 2)
    return pl.pallas_call(
        functools.partial(_moba_kernel, nb=nb),
        grid=(b, npair),
        in_specs=[pl.BlockSpec((1, s, LANES), lambda i, p: (i, 0, p)),
                  pl.BlockSpec((1, s, LANES), lambda i, p: (i, 0, npair + p)),
                  pl.BlockSpec((1, s, LANES), lambda i, p: (i, 0, 2 * npair + p))],
        out_specs=pl.BlockSpec((1, s, LANES), lambda i, p: (i, 0, p)),
        out_shape=jax.ShapeDtypeStruct((b, s, ATT_HEADS * HEAD_DIM), BF16),
        compiler_params=pltpu.CompilerParams(dimension_semantics=("arbitrary", "arbitrary"),
                                             vmem_limit_bytes=_vmem_limit(est)),
        name="moba",
    )(qkv, qkv, qkv)


_STG_KB, _STG_QN, _STG_KN, _STG_VB, _STG_KBE, _STG_QD = range(6)


def _gdn_kernel(x_ref, sm_ref, cw_ref, alog_ref, dtb_ref, utri_ref, exp_ref, hsum_ref, o_ref,
                xs_ref, st_ref, stg_ref, gx_ref, *, rows, groups_per_seq):
    ck = DN_CHUNK
    dn_w = DN_HEADS * HEAD_DIM
    npair = DN_HEADS // 2
    t = pl.program_id(0)
    halo = 8

    @pl.when(t == 0)
    def _():
        xs_ref[0:halo, :] = jnp.zeros((halo, 3 * dn_w), F32)
        st_ref[...] = jnp.zeros_like(st_ref)
        stg_ref[...] = jnp.zeros_like(stg_ref)
        gx_ref[...] = jnp.zeros_like(gx_ref)

    ri = lax.broadcasted_iota(jnp.int32, (ck, dn_w), 0)
    ci = jnp.bitwise_and(lax.broadcasted_iota(jnp.int32, (ck, dn_w), 1), ck - 1)
    tri, strict, diag = ri >= ci, ri > ci, ri == ci
    r2 = lax.broadcasted_iota(jnp.int32, (LANES, LANES), 0) // HEAD_DIM
    c2 = lax.broadcasted_iota(jnp.int32, (LANES, LANES), 1) // HEAD_DIM
    bmask = r2 == c2
    eye2 = jnp.where(diag[:, :LANES], 1.0, 0.0)

    def bd2(a):
        a = a.astype(BF16)
        return jnp.where(bmask, jnp.concatenate([a, a], axis=0), jnp.zeros((LANES, LANES), BF16))

    nchunk = rows // ck
    chains = [(cc, p) for cc in range(nchunk) for p in range(npair)]
    rs = lambda cc: slice(cc * ck, (cc + 1) * ck)
    ls = lambda p: slice(p * LANES, (p + 1) * LANES)
    staged = lambda which, cc, p: stg_ref[which, rs(cc), ls(p)]

    gch = [gx_ref[rs(cc), :] for cc in range(nchunk)]
    glast = [g_[ck - 1:ck, :] for g_ in gch]
    dec, kdec, gl_exp = [], [], []
    for cc, g_ in enumerate(gch):
        grow = jnp.sum(jnp.where(diag, g_, 0.0), axis=0, keepdims=True)
        dec.append(jnp.where(tri, jnp.exp(jnp.where(tri, g_ - grow, 0.0)), 0.0))
        kdec.append((stg_ref[_STG_KN, rs(cc), :].astype(F32) * jnp.exp(glast[cc] - g_)).astype(BF16))
        gl_exp.append(jnp.exp(glast[cc]))
    qd_old = [stg_ref[_STG_QD, rs(cc), :] for cc in range(nchunk)]

    seq_start = lax.rem(t, groups_per_seq) == 0
    cw = cw_ref[...]
    hsum2 = hsum_ref[0:LANES, 0:LANES]
    act, sumsq = {}, {}

    def prepare(slab):
        cs = ls(slab)
        xs_ref[0:halo, cs] = jnp.where(seq_start, 0.0, xs_ref[0:halo, cs])
        xs_ref[halo:halo + rows, cs] = x_ref[0, :, cs].astype(F32)
        xf = xs_ref[:, cs]
        y = xf * cw[CONV_WIDTH - 1:CONV_WIDTH, cs]
        for j in range(1, CONV_WIDTH):
            y = y + pltpu.roll(xf, j, axis=0) * cw[CONV_WIDTH - 1 - j:CONV_WIDTH - j, cs]
        y = y[halo:]
        xs_ref[0:halo, cs] = xf[rows:]
        y = y * _sigmoid(y)
        act[slab] = y
        if slab < 2 * npair:
            sumsq[slab] = _dot((y * y).astype(BF16), hsum2)

    def stage(p):
        qn = act[p] * (lax.rsqrt(sumsq[p] + NORM_EPS) * (HEAD_DIM ** -0.5))
        kn = act[npair + p] * lax.rsqrt(sumsq[npair + p] + NORM_EPS)
        bx, eg = bexp[:, ls(p)], jnp.exp(gexp[:, ls(p)])
        kb = kn * bx
        stg_ref[_STG_KB, :, ls(p)] = kb.astype(BF16)
        stg_ref[_STG_QN, :, ls(p)] = qn.astype(BF16)
        stg_ref[_STG_KN, :, ls(p)] = kn.astype(BF16)
        stg_ref[_STG_VB, :, ls(p)] = (act[2 * npair + p] * bx).astype(BF16)
        stg_ref[_STG_KBE, :, ls(p)] = (kb * eg).astype(BF16)
        stg_ref[_STG_QD, :, ls(p)] = (qn * eg).astype(BF16)

    todo = list(range(3 * npair))

    def prepare_some(n):
        for _ in range(n):
            if todo:
                prepare(todo.pop(0))

    kq = {(cc, p): lax.dot_general(jnp.concatenate([staged(_STG_KB, cc, p), staged(_STG_QN, cc, p)], axis=0),
                                   bd2(staged(_STG_KN, cc, p)), _NT, preferred_element_type=F32)
          for cc, p in chains}
    vbd = {(cc, p): jnp.concatenate([bd2(staged(_STG_VB, cc, p)), bd2(staged(_STG_KBE, cc, p))], axis=1)
           for cc, p in chains}
    sm = sm_ref[...]
    widen = lambda a: jnp.concatenate([a] * (rows // LANES), axis=1)
    xa = sm + widen(dtb_ref[...])
    softplus = jnp.maximum(xa, 0.0) + jnp.log(1.0 + jnp.exp(-jnp.abs(xa)))
    g = -jnp.exp(widen(alog_ref[...])) * softplus
    utri = utri_ref[...]
    gc = sum(_dot(t_, utri) for t_ in _split3(g))
    prepare_some(2)
    neg_l = {(cc, p): -jnp.where(strict[:, ls(p)], kq[cc, p][:ck] * dec[cc][:, ls(p)], 0.0) for cc, p in chains}
    qk = {(cc, p): jnp.where(tri[:, ls(p)], kq[cc, p][ck:] * dec[cc][:, ls(p)], 0.0) for cc, p in chains}
    ssum = {ch: eye2 + neg_l[ch] for ch in chains}
    pw = {ch: _dot(neg_l[ch].astype(BF16), bd2(neg_l[ch])) for ch in chains}
    gate_row = lax.broadcasted_iota(jnp.int32, (2 * DN_HEADS, rows), 0)
    comb = jnp.where(gate_row < DN_HEADS, _sigmoid(sm), gc)
    ex = lax.dot_general(jnp.concatenate(_split3(comb), axis=0), exp_ref[...], _TN, preferred_element_type=F32)
    bexp, gexp = ex[:, :dn_w], ex[:, dn_w:]
    prepare_some(2)
    span = 2
    while span * 2 < ck:
        both = {ch: _dot(jnp.concatenate([ssum[ch], pw[ch]], axis=0).astype(BF16), bd2(pw[ch])) for ch in chains}
        prepare_some(2)
        ssum = {ch: ssum[ch] + both[ch][:ck] for ch in chains}
        pw = {ch: both[ch][ck:] for ch in chains}
        span *= 2
    corr = {ch: _dot(ssum[ch].astype(BF16), bd2(pw[ch])) for ch in chains}
    prepare_some(len(todo))
    uw = {ch: _dot((ssum[ch] + corr[ch]).astype(BF16), vbd[ch]) for ch in chains}

    fresh = lax.rem(jnp.maximum(t - 1, 0), groups_per_seq) == 0
    state = [jnp.where(fresh, 0.0, st_ref[p]) for p in range(npair)]
    for cc in range(nchunk):
        wq = [_dot(jnp.concatenate([uw[cc, p][:, LANES:].astype(BF16), qd_old[cc][:, ls(p)]], axis=0),
                   state[p].astype(BF16)) for p in range(npair)]
        if cc < npair:
            stage(cc)
        vnew = [uw[cc, p][:, :LANES] - wq[p][:ck] for p in range(npair)]
        intra = [_dot(qk[cc, p].astype(BF16), bd2(vnew[p])) for p in range(npair)]
        upd = [lax.dot_general(kdec[cc][:, ls(p)], vnew[p].astype(BF16), _TN, preferred_element_type=F32)
               for p in range(npair)]
        for p in range(npair):
            o_ref[0, rs(cc), ls(p)] = (wq[p][ck:] + intra[p]).astype(o_ref.dtype)
        state = [state[p] * gl_exp[cc][:, ls(p)] + jnp.where(bmask, upd[p], 0.0) for p in range(npair)]
    for p in range(npair):
        st_ref[p] = state[p]
    for p in range(nchunk, npair):
        stage(p)
    gx_ref[...] = gexp


def _gdn(qkv_dn, small, conv_w, a_log, dt_bias, *, rows):
    b, s, w3 = qkv_dn.shape
    dn_w = DN_HEADS * HEAD_DIM
    assert w3 == 3 * dn_w and s % rows == 0 and rows % DN_CHUNK == 0
    h = DN_HEADS
    nc = s // rows
    nsteps = b * nc
    assert small.shape == (2 * h, b * s) and rows % LANES == 0
    zeros = jnp.zeros((h,), F32)
    alog_col = jnp.tile(jnp.concatenate([zeros, a_log.astype(F32)])[:, None], (1, LANES))
    dtb_col = jnp.tile(jnp.concatenate([zeros, dt_bias.astype(F32)])[:, None], (1, LANES))
    r = jnp.arange(rows)
    utri = ((r[:, None] // DN_CHUNK == r[None, :] // DN_CHUNK) & (r[:, None] <= r[None, :])).astype(BF16)
    src = jnp.arange(2 * h)[:, None]
    dst = jnp.arange(2 * dn_w)[None, :]
    expander = ((dst // dn_w == src // h) & ((dst % dn_w) // HEAD_DIM == src % h)).astype(BF16)
    expander = jnp.tile(expander, (3, 1))
    hl = jnp.arange(dn_w) // HEAD_DIM
    hsum = (hl[:, None] == hl[None, :]).astype(BF16)
    const = lambda a: pl.BlockSpec(a.shape, lambda t: (0,) * a.ndim)

    def produced(t):
        g = jnp.minimum(t, nsteps - 1)
        return g // nc, g % nc, 0

    def consumed(t):
        g = jnp.maximum(t - 1, 0)
        return g // nc, g % nc, 0

    est = (2 * (rows * w3 * 2 + 2 * h * rows * 4 + rows * dn_w * 4) + (rows + 8) * w3 * 4 + 12 * rows * w3 * 4
           + 6 * rows * dn_w * 2 + rows * dn_w * 4 + 2 * (utri.size + expander.size + hsum.size) * 2)
    return pl.pallas_call(
        functools.partial(_gdn_kernel, rows=rows, groups_per_seq=nc),
        grid=(nsteps + 1,),
        in_specs=[pl.BlockSpec((1, rows, w3), produced),
                  pl.BlockSpec((2 * h, rows), lambda t: (0, jnp.minimum(t, nsteps - 1))),
                  const(conv_w), const(alog_col), const(dtb_col), const(utri), const(expander), const(hsum)],
        out_specs=pl.BlockSpec((1, rows, dn_w), consumed),
        out_shape=jax.ShapeDtypeStruct((b, s, dn_w), F32),
        scratch_shapes=[pltpu.VMEM((rows + 8, w3), F32), pltpu.VMEM((DN_HEADS // 2, LANES, LANES), F32),
                        pltpu.VMEM((6, rows, dn_w), BF16), pltpu.VMEM((rows, dn_w), F32)],
        compiler_params=pltpu.CompilerParams(dimension_semantics=("arbitrary",),
                                             vmem_limit_bytes=_vmem_limit(est)),
        name="gdn",
    )(qkv_dn, small, conv_w.astype(F32), alog_col, dtb_col, utri, expander, hsum)


def _mixout_kernel(x_ref, ya_ref, od_ref, z_ref, gate_ref, dnw_ref, hmean_ref, wa_ref, wd_ref, wo_ref, pn_ref, o_ref,
                   *, sub_rows):
    tm, d = x_ref.shape
    subs = [slice(r, r + sub_rows) for r in range(0, tm, sub_rows)]
    od = [od_ref[s, :] for s in subs]
    ms = [_dot((o * o).astype(BF16), hmean_ref[...]) for o in od]
    att = [_dot(ya_ref[s, :], wa_ref[...]) for s in subs]
    dn = []
    for s, o, m in zip(subs, od, ms):
        z = z_ref[s, :].astype(F32)
        y_dn = o * lax.rsqrt(m + NORM_EPS) * dnw_ref[...] * (z * _sigmoid(z))
        dn.append(_dot(y_dn.astype(BF16), wd_ref[...]))
    ys = []
    for s, a, b in zip(subs, att, dn):
        merged = _sigmoid(gate_ref[s, :d].astype(F32)) * a + _sigmoid(gate_ref[s, d:].astype(F32)) * b
        ys.append(_dot(merged.astype(BF16), wo_ref[...]))
    for s, y in zip(subs, ys):
        o_ref[s, :] = x_ref[s, :] + _rms(y, pn_ref[...])


def _mixout(x2, y_att, o_dn, z, gates, dn_norm, wa, wd, wo, post_norm, *, tm):
    t, d = x2.shape
    dn_w = DN_HEADS * HEAD_DIM
    dnw_row = jnp.tile(dn_norm.astype(F32), DN_HEADS)[None, :]
    hl = jnp.arange(dn_w) // HEAD_DIM
    hmean = ((hl[:, None] == hl[None, :]).astype(F32) / HEAD_DIM).astype(BF16)
    row = lambda w: pl.BlockSpec((tm, w), lambda i: (i, 0))
    const = lambda a: pl.BlockSpec(a.shape, lambda i: (0, 0), pipeline_mode=pl.Buffered(1))
    est = (2 * (2 * tm * d * 4 + tm * dn_w * (2 + 4 + 2) + tm * 2 * d * 2)
           + (hmean.size + wa.size + wd.size + wo.size) * 2 + 8 * tm * d * 4)
    return pl.pallas_call(
        functools.partial(_mixout_kernel, sub_rows=min(tm, 256)),
        grid=(t // tm,),
        in_specs=[row(d), row(dn_w), row(dn_w), row(dn_w), row(2 * d), const(dnw_row), const(hmean),
                  const(wa), const(wd), const(wo), const(post_norm)],
        out_specs=row(d),
        out_shape=jax.ShapeDtypeStruct((t, d), F32),
        compiler_params=pltpu.CompilerParams(dimension_semantics=("arbitrary",), vmem_limit_bytes=_vmem_limit(est)),
        name="mixout",
    )(x2, y_att, o_dn, z, gates, dnw_row, hmean, wa, wd, wo, post_norm)


def _mlp_kernel(x_ref, pre_ref, w1_ref, w2_ref, post_ref, o_ref, *, ff_chunk):
    x = x_ref[...]
    hb = _rms(x, pre_ref[...]).astype(BF16)
    acc = jnp.zeros(x.shape, F32)
    for c in range(0, w1_ref.shape[1], ff_chunk):
        a = jnp.maximum(_dot(hb, w1_ref[:, c:c + ff_chunk]), 0.0)
        acc = acc + _dot((a * a).astype(BF16), w2_ref[c:c + ff_chunk, :])
    o_ref[...] = x + _rms(acc, post_ref[...])


def _mlp(x1, pre, w1, w2, post, *, tm):
    t, d = x1.shape
    row = pl.BlockSpec((tm, d), lambda i: (i, 0))
    const = lambda a: pl.BlockSpec(a.shape, lambda i: (0, 0), pipeline_mode=pl.Buffered(1))
    est = 2 * (2 * tm * d * 4) + (w1.size + w2.size) * 2 + 6 * tm * d * 4 + 2 * tm * 1024 * 4
    return pl.pallas_call(
        functools.partial(_mlp_kernel, ff_chunk=1024),
        grid=(t // tm,),
        in_specs=[row, const(pre), const(w1), const(w2), const(post)],
        out_specs=row,
        out_shape=jax.ShapeDtypeStruct((t, d), F32),
        compiler_params=pltpu.CompilerParams(dimension_semantics=("arbitrary",), vmem_limit_bytes=_vmem_limit(est)),
        name="mlp",
    )(x1, pre, w1, w2, post)


def kernel(x, pre_norm_mix, w_in, conv_w, a_log, dt_bias, dn_norm, w_branch_att, w_branch_dn,
           w_out, post_norm_mix, pre_norm_mlp, w_mlp_in, w_mlp_out, post_norm_mlp):
    b, s, d = x.shape
    att_w = ATT_HEADS * HEAD_DIM
    dn_w = DN_HEADS * HEAD_DIM
    n_main = 3 * att_w + 3 * dn_w + dn_w
    tm = 1024
    x2 = x.reshape(b * s, d)
    for l in range(w_in.shape[0]):
        wl = w_in[l]
        q_scale = HEAD_DIM ** -0.5 * LOG2_E
        w_main = jnp.concatenate([wl[:, :att_w] * q_scale, wl[:, att_w:n_main], wl[:, n_main + 2 * DN_HEADS:]],
                                 axis=1).astype(BF16)
        w_small = wl[:, n_main:n_main + 2 * DN_HEADS].T.astype(BF16)
        qkv_att, qkv_dn, z, gates, small = _inproj(x2, pre_norm_mix[l][None, :], w_main, w_small, tm=tm)
        y_att = _moba(qkv_att.reshape(b, s, 3 * att_w))
        o_dn = _gdn(qkv_dn.reshape(b, s, 3 * dn_w), small, conv_w[l], a_log[l], dt_bias[l], rows=256)
        x2 = _mixout(x2, y_att.reshape(b * s, att_w), o_dn.reshape(b * s, dn_w), z, gates, dn_norm[l],
                     w_branch_att[l].astype(BF16), w_branch_dn[l].astype(BF16), w_out[l].astype(BF16),
                     post_norm_mix[l][None, :], tm=tm)
        x2 = _mlp(x2, pre_norm_mlp[l][None, :], w_mlp_in[l].astype(BF16), w_mlp_out[l].astype(BF16),
                  post_norm_mlp[l][None, :], tm=tm)
    return x2.reshape(b, s, d)
```

```python
import functools

import jax
import jax.numpy as jnp
from jax import lax
from jax.experimental import pallas as pl
from jax.experimental.pallas import tpu as pltpu

ATT_HEADS = 8
DN_HEADS = 8
HEAD_DIM = 64
MOBA_BLOCK = 256
MOBA_TOPK = 3
DN_CHUNK = 64
CONV_WIDTH = 4
NORM_EPS = 1e-6

LANES = 128
V7X_VMEM_BYTES = 64 * 1024 * 1024
VMEM_CAP_BYTES = V7X_VMEM_BYTES - 8 * 1024 * 1024

F32 = jnp.float32
BF16 = jnp.bfloat16
NEG_BIG = -1e30
MASK_BIAS = -(2.0 ** 100)
LOG2_E = 1.4426950408889634

_NT = (((1,), (1,)), ((), ()))
_TN = (((0,), (0,)), ((), ()))


def _vmem_limit(nbytes):
    return int(min(VMEM_CAP_BYTES, nbytes * 5 // 4 + (4 << 20)))


def _dot(a, b):
    return jnp.dot(a, b, preferred_element_type=F32)


def _rms(x, w):
    return x * lax.rsqrt(jnp.mean(x * x, axis=-1, keepdims=True) + NORM_EPS) * w


def _sigmoid(x):
    return 1.0 / (1.0 + jnp.exp(-x))


def _split3(x):
    hi = x.astype(BF16)
    r = x - hi.astype(F32)
    mid = r.astype(BF16)
    lo = (r - mid.astype(F32)).astype(BF16)
    return hi, mid, lo


def _inproj_kernel(x_ref, g_ref, wm_ref, wg_ref, ws_ref, att_ref, dn_ref, z_ref, gate_ref, small_ref, *, col_chunk):
    hb = _rms(x_ref[...], g_ref[...]).astype(BF16)
    start = 0
    for ref in (att_ref, dn_ref, z_ref):
        width = ref.shape[1]
        for c in range(0, width, col_chunk):
            ref[:, c:c + col_chunk] = _dot(hb, wm_ref[:, start + c:start + c + col_chunk]).astype(ref.dtype)
        start += width
    for c in range(0, gate_ref.shape[1], col_chunk):
        gate_ref[:, c:c + col_chunk] = _dot(hb, wg_ref[:, c:c + col_chunk]).astype(gate_ref.dtype)
    small_ref[...] = lax.dot_general(ws_ref[...], hb, _NT, preferred_element_type=F32)


def _inproj(x2, gain, w_main, w_gate, w_small, *, tm):
    t, d = x2.shape
    att_w, dn_w, z_w, gate_w = 3 * ATT_HEADS * HEAD_DIM, 3 * DN_HEADS * HEAD_DIM, DN_HEADS * HEAD_DIM, 2 * d
    assert w_main.shape == (d, att_w + dn_w + z_w) and w_gate.shape == (d, gate_w) and t % tm == 0
    n_out = w_main.shape[1] + gate_w
    row = lambda w: pl.BlockSpec((tm, w), lambda i: (i, 0))
    const = lambda shp: pl.BlockSpec(shp, lambda i: (0, 0), pipeline_mode=pl.Buffered(1))
    est = (2 * (tm * d * 4 + tm * n_out * 2 + tm * LANES * 4) + (w_main.size + w_gate.size + w_small.size) * 2
           + 3 * tm * d * 4 + 2 * tm * 512 * 4)
    return pl.pallas_call(
        functools.partial(_inproj_kernel, col_chunk=512),
        grid=(t // tm,),
        in_specs=[row(d), const((1, d)), const(w_main.shape), const(w_gate.shape), const(w_small.shape)],
        out_specs=[row(att_w), row(dn_w), row(z_w), row(gate_w), pl.BlockSpec((w_small.shape[0], tm), lambda i: (0, i))],
        out_shape=[jax.ShapeDtypeStruct((t, att_w), BF16), jax.ShapeDtypeStruct((t, dn_w), BF16),
                   jax.ShapeDtypeStruct((t, z_w), BF16), jax.ShapeDtypeStruct((t, gate_w), BF16),
                   jax.ShapeDtypeStruct((w_small.shape[0], t), F32)],
        compiler_params=pltpu.CompilerParams(dimension_semantics=("arbitrary",), vmem_limit_bytes=_vmem_limit(est)),
        name="inproj",
    )(x2, gain, w_main, w_gate, w_small)


def _moba_kernel(q_ref, k_ref, v_ref, o_ref, *, nb):
    blk = MOBA_BLOCK
    lane = lax.broadcasted_iota(jnp.int32, (blk, LANES), 1)
    first_head = lane < HEAD_DIM

    km = jnp.concatenate(
        [jnp.sum(k_ref[0, n * blk:(n + 1) * blk, :].astype(F32), axis=0, keepdims=True) for n in range(nb)]
        + [jnp.zeros((8 - nb, LANES), F32)] * (1 if nb < 8 else 0), axis=0) * (1.0 / blk)
    l8 = lax.broadcasted_iota(jnp.int32, (8, LANES), 1) < HEAD_DIM
    pad = jnp.zeros((HEAD_DIM - 8, LANES), F32)
    g = jnp.concatenate([jnp.where(l8, 0.0, km), pad, jnp.where(l8, km, 0.0), pad], axis=0)
    g_hi = g.astype(BF16)
    g_lo = (g - g_hi.astype(F32)).astype(BF16)

    s_len = nb * blk
    kblk = lax.broadcasted_iota(jnp.int32, (s_len, LANES), 0) // blk
    klane = lax.broadcasted_iota(jnp.int32, (s_len, LANES), 1)
    k2 = k_ref[0]
    k_aug = (jnp.where(klane < HEAD_DIM, k2, jnp.where(klane - HEAD_DIM == kblk, 1.0, 0.0).astype(BF16)),
             jnp.where(klane < HEAD_DIM, jnp.where(klane == kblk, 1.0, 0.0).astype(BF16), k2))

    row = lax.broadcasted_iota(jnp.int32, (8, blk), 0)

    def drop_bias(s, j):
        out = jnp.zeros((8, blk), F32)
        past = row < j
        for n in range(j):
            rn = s[n:n + 1, :]
            beats = jnp.where(s > rn, 1.0, jnp.where(jnp.logical_and(s == rn, row < n), 1.0, 0.0))
            cnt = jnp.sum(jnp.where(past, beats, 0.0), axis=0, keepdims=True)
            out = jnp.where(row == n, jnp.where(cnt < float(MOBA_TOPK), 0.0, MASK_BIAS), out)
        return out

    def queries(j):
        q2 = q_ref[0, j * blk:(j + 1) * blk, :]
        qs = q2
        zero = jnp.zeros_like(qs)
        if j <= MOBA_TOPK:
            return jnp.where(first_head, qs, zero), jnp.where(first_head, zero, qs)
        s_t = (lax.dot_general(g_hi, q2, _NT, preferred_element_type=F32)
               + lax.dot_general(g_lo, q2, _NT, preferred_element_type=F32))
        padq = jnp.zeros((HEAD_DIM - 8, blk), F32)
        bias_t = jnp.concatenate([drop_bias(s_t[0:8], j), padq, drop_bias(s_t[HEAD_DIM:HEAD_DIM + 8], j), padq], axis=0)
        bias = bias_t.T.astype(BF16)
        return jnp.where(first_head, qs, bias), jnp.where(first_head, bias, qs)

    def scores(j0, j1):
        q0 = queries(j0)
        q1 = queries(j1) if j1 is not None else None
        out = []
        for hd in range(2):
            keys = k_aug[hd][0:(j0 + 1) * blk]
            if j1 is None:
                out.append((lax.dot_general(q0[hd], keys, _NT, preferred_element_type=F32), None))
            else:
                out.append((lax.dot_general(jnp.concatenate([q0[hd], q1[hd]], axis=0), keys, _NT, preferred_element_type=F32),
                            lax.dot_general(q1[hd], k_aug[hd][j1 * blk:(j1 + 1) * blk], _NT, preferred_element_type=F32)))
        return out

    tri = lax.broadcasted_iota(jnp.int32, (blk, blk), 1) <= lax.broadcasted_iota(jnp.int32, (blk, blk), 0)

    def softmax_terms(parts):
        m = jnp.max(parts[0], axis=1, keepdims=True)
        for pt in parts[1:]:
            m = jnp.maximum(m, jnp.max(pt, axis=1, keepdims=True))
        return [jnp.exp2(pt - m).astype(BF16) for pt in parts]

    def normalised(pv):
        return pv / pltpu.roll(pv, HEAD_DIM, axis=1)

    v2 = v_ref[0]
    ones = jnp.ones_like(v2)
    v_aug = (jnp.where(klane < HEAD_DIM, v2, ones), jnp.where(klane < HEAD_DIM, ones, v2))
    groups = [(j0, j0 + 1 if j0 + 1 < nb else None) for j0 in range(0, nb, 2)]
    sc = scores(*groups[0])
    for gi, (j0, j1) in enumerate(groups):
        nxt = scores(*groups[gi + 1]) if gi + 1 < len(groups) else None
        w0 = (j0 + 1) * blk
        outs0, outs1 = [], []
        for hd in range(2):
            shared, own1 = sc[hd]
            pr0 = softmax_terms([shared[:blk, n * blk:(n + 1) * blk] for n in range(j0)]
                                + [jnp.where(tri, shared[:blk, j0 * blk:w0], NEG_BIG)])
            if j1 is None:
                outs0.append(normalised(_dot(jnp.concatenate(pr0, axis=1), v_aug[hd][0:w0])))
                continue
            pr1 = softmax_terms([shared[blk:, n * blk:(n + 1) * blk] for n in range(j0 + 1)]
                                + [jnp.where(tri, own1, NEG_BIG)])
            both = jnp.concatenate([jnp.concatenate(pr0, axis=1), jnp.concatenate(pr1[:-1], axis=1)], axis=0)
            pv = _dot(both, v_aug[hd][0:w0])
            outs0.append(normalised(pv[:blk]))
            outs1.append(normalised(pv[blk:] + _dot(pr1[-1], v_aug[hd][j1 * blk:(j1 + 1) * blk])))
        o_ref[0, j0 * blk:(j0 + 1) * blk, :] = jnp.where(first_head, outs0[0], outs0[1]).astype(o_ref.dtype)
        if j1 is not None:
            o_ref[0, j1 * blk:(j1 + 1) * blk, :] = jnp.where(first_head, outs1[0], outs1[1]).astype(o_ref.dtype)
        sc = nxt


def _moba(qkv):
    b, s, w3 = qkv.shape
    blk = MOBA_BLOCK
    npair = ATT_HEADS // 2
    assert w3 == 3 * ATT_HEADS * HEAD_DIM and s % blk == 0 and s // blk <= 8
    nb = s // blk
    est = 2 * 4 * s * LANES * 2 + 8 * blk * s * (4 + 4 + 2)
    return pl.pallas_call(
        functools.partial(_moba_kernel, nb=nb),
        grid=(b, npair),
        in_specs=[pl.BlockSpec((1, s, LANES), lambda i, p: (i, 0, p)),
                  pl.BlockSpec((1, s, LANES), lambda i, p: (i, 0, npair + p)),
                  pl.BlockSpec((1, s, LANES), lambda i, p: (i, 0, 2 * npair + p))],
        out_specs=pl.BlockSpec((1, s, LANES), lambda i, p: (i, 0, p)),
        out_shape=jax.ShapeDtypeStruct((b, s, ATT_HEADS * HEAD_DIM), BF16),
        compiler_params=pltpu.CompilerParams(dimension_semantics=("arbitrary", "arbitrary"),
                                             vmem_limit_bytes=_vmem_limit(est)),
        name="moba",
    )(qkv, qkv, qkv)


_STG_KB, _STG_QN, _STG_KN, _STG_VB, _STG_KBE, _STG_QD = range(6)


def _gdn_kernel(x_ref, sm_ref, cw_ref, alog_ref, dtb_ref, utri_ref, exp_ref, hsum_ref, o_ref,
                xs_ref, st_ref, stg_ref, gx_ref, *, rows, groups_per_seq):
    ck = DN_CHUNK
    dn_w = DN_HEADS * HEAD_DIM
    npair = DN_HEADS // 2
    t = pl.program_id(0)
    halo = 8

    @pl.when(t == 0)
    def _():
        xs_ref[0:halo, :] = jnp.zeros((halo, 3 * dn_w), F32)
        st_ref[...] = jnp.zeros_like(st_ref)
        stg_ref[...] = jnp.zeros_like(stg_ref)
        gx_ref[...] = jnp.zeros_like(gx_ref)

    ri = lax.broadcasted_iota(jnp.int32, (ck, dn_w), 0)
    ci = jnp.bitwise_and(lax.broadcasted_iota(jnp.int32, (ck, dn_w), 1), ck - 1)
    tri, strict, diag = ri >= ci, ri > ci, ri == ci
    r2 = lax.broadcasted_iota(jnp.int32, (LANES, LANES), 0) // HEAD_DIM
    c2 = lax.broadcasted_iota(jnp.int32, (LANES, LANES), 1) // HEAD_DIM
    bmask = r2 == c2
    eye2 = jnp.where(diag[:, :LANES], 1.0, 0.0)

    def bd2(a):
        a = a.astype(BF16)
        return jnp.where(bmask, jnp.concatenate([a, a], axis=0), jnp.zeros((LANES, LANES), BF16))

    nchunk = rows // ck
    chains = [(cc, p) for cc in range(nchunk) for p in range(npair)]
    rs = lambda cc: slice(cc * ck, (cc + 1) * ck)
    ls = lambda p: slice(p * LANES, (p + 1) * LANES)
    staged = lambda which, cc, p: stg_ref[which, rs(cc), ls(p)]

    gch = [gx_ref[rs(cc), :] for cc in range(nchunk)]
    glast = [g_[ck - 1:ck, :] for g_ in gch]
    dec, kdec, gl_exp = [], [], []
    for cc, g_ in enumerate(gch):
        grow = jnp.sum(jnp.where(diag, g_, 0.0), axis=0, keepdims=True)
        dec.append(jnp.where(tri, jnp.exp(jnp.where(tri, g_ - grow, 0.0)), 0.0))
        kdec.append((stg_ref[_STG_KN, rs(cc), :].astype(F32) * jnp.exp(glast[cc] - g_)).astype(BF16))
        gl_exp.append(jnp.exp(glast[cc]))
    qd_old = [stg_ref[_STG_QD, rs(cc), :] for cc in range(nchunk)]

    seq_start = lax.rem(t, groups_per_seq) == 0
    cw = cw_ref[...]
    hsum2 = hsum_ref[0:LANES, 0:LANES]
    act, sumsq = {}, {}

    def prepare(slab):
        cs = ls(slab)
        xs_ref[0:halo, cs] = jnp.where(seq_start, 0.0, xs_ref[0:halo, cs])
        xs_ref[halo:halo + rows, cs] = x_ref[0, :, cs].astype(F32)
        xf = xs_ref[:, cs]
        y = xf * cw[CONV_WIDTH - 1:CONV_WIDTH, cs]
        for j in range(1, CONV_WIDTH):
            y = y + pltpu.roll(xf, j, axis=0) * cw[CONV_WIDTH - 1 - j:CONV_WIDTH - j, cs]
        y = y[halo:]
        xs_ref[0:halo, cs] = xf[rows:]
        y = y * _sigmoid(y)
        act[slab] = y
        if slab < 2 * npair:
            sumsq[slab] = _dot((y * y).astype(BF16), hsum2)

    def stage(p):
        qn = act[p] * (lax.rsqrt(sumsq[p] + NORM_EPS) * (HEAD_DIM ** -0.5))
        kn = act[npair + p] * lax.rsqrt(sumsq[npair + p] + NORM_EPS)
        bx, eg = bexp[:, ls(p)], jnp.exp(gexp[:, ls(p)])
        kb = kn * bx
        stg_ref[_STG_KB, :, ls(p)] = kb.astype(BF16)
        stg_ref[_STG_QN, :, ls(p)] = qn.astype(BF16)
        stg_ref[_STG_KN, :, ls(p)] = kn.astype(BF16)
        stg_ref[_STG_VB, :, ls(p)] = (act[2 * npair + p] * bx).astype(BF16)
        stg_ref[_STG_KBE, :, ls(p)] = (kb * eg).astype(BF16)
        stg_ref[_STG_QD, :, ls(p)] = (qn * eg).astype(BF16)

    todo = list(range(3 * npair))

    def prepare_some(n):
        for _ in range(n):
            if todo:
                prepare(todo.pop(0))

    kq = {(cc, p): lax.dot_general(jnp.concatenate([staged(_STG_KB, cc, p), staged(_STG_QN, cc, p)], axis=0),
                                   bd2(staged(_STG_KN, cc, p)), _NT, preferred_element_type=F32)
          for cc, p in chains}
    vbd = {(cc, p): jnp.concatenate([bd2(staged(_STG_VB, cc, p)), bd2(staged(_STG_KBE, cc, p))], axis=1)
           for cc, p in chains}
    sm = sm_ref[...]
    widen = lambda a: jnp.concatenate([a] * (rows // LANES), axis=1)
    xa = sm + widen(dtb_ref[...])
    softplus = jnp.maximum(xa, 0.0) + jnp.log(1.0 + jnp.exp(-jnp.abs(xa)))
    g = -jnp.exp(widen(alog_ref[...])) * softplus
    utri = utri_ref[...]
    gc = sum(_dot(t_, utri) for t_ in _split3(g))
    prepare_some(2)
    neg_l = {(cc, p): -jnp.where(strict[:, ls(p)], kq[cc, p][:ck] * dec[cc][:, ls(p)], 0.0) for cc, p in chains}
    qk = {(cc, p): jnp.where(tri[:, ls(p)], kq[cc, p][ck:] * dec[cc][:, ls(p)], 0.0) for cc, p in chains}
    ssum = {ch: eye2 + neg_l[ch] for ch in chains}
    pw = {ch: _dot(neg_l[ch].astype(BF16), bd2(neg_l[ch])) for ch in chains}
    gate_row = lax.broadcasted_iota(jnp.int32, (2 * DN_HEADS, rows), 0)
    comb = jnp.where(gate_row < DN_HEADS, _sigmoid(sm), gc)
    ex = lax.dot_general(jnp.concatenate(_split3(comb), axis=0), exp_ref[...], _TN, preferred_element_type=F32)
    bexp, gexp = ex[:, :dn_w], ex[:, dn_w:]
    prepare_some(2)
    span = 2
    while span * 2 < ck:
        both = {ch: _dot(jnp.concatenate([ssum[ch], pw[ch]], axis=0).astype(BF16), bd2(pw[ch])) for ch in chains}
        prepare_some(2)
        ssum = {ch: ssum[ch] + both[ch][:ck] for ch in chains}
        pw = {ch: both[ch][ck:] for ch in chains}
        span *= 2
    corr = {ch: _dot(ssum[ch].astype(BF16), bd2(pw[ch])) for ch in chains}
    prepare_some(len(todo))
    uw = {ch: _dot((ssum[ch] + corr[ch]).astype(BF16), vbd[ch]) for ch in chains}

    fresh = lax.rem(jnp.maximum(t - 1, 0), groups_per_seq) == 0
    state = [jnp.where(fresh, 0.0, st_ref[p]) for p in range(npair)]
    for cc in range(nchunk):
        wq = [_dot(jnp.concatenate([uw[cc, p][:, LANES:].astype(BF16), qd_old[cc][:, ls(p)]], axis=0),
                   state[p].astype(BF16)) for p in range(npair)]
        if cc < npair:
            stage(cc)
        vnew = [uw[cc, p][:, :LANES] - wq[p][:ck] for p in range(npair)]
        intra = [_dot(qk[cc, p].astype(BF16), bd2(vnew[p])) for p in range(npair)]
        upd = [lax.dot_general(kdec[cc][:, ls(p)], vnew[p].astype(BF16), _TN, preferred_element_type=F32)
               for p in range(npair)]
        for p in range(npair):
            o_ref[0, rs(cc), ls(p)] = (wq[p][ck:] + intra[p]).astype(o_ref.dtype)
        state = [state[p] * gl_exp[cc][:, ls(p)] + jnp.where(bmask, upd[p], 0.0) for p in range(npair)]
    for p in range(npair):
        st_ref[p] = state[p]
    for p in range(nchunk, npair):
        stage(p)
    gx_ref[...] = gexp


def _gdn(qkv_dn, small, conv_w, a_log, dt_bias, *, rows):
    b, s, w3 = qkv_dn.shape
    dn_w = DN_HEADS * HEAD_DIM
    assert w3 == 3 * dn_w and s % rows == 0 and rows % DN_CHUNK == 0
    h = DN_HEADS
    nc = s // rows
    nsteps = b * nc
    assert small.shape == (2 * h, b * s) and rows % LANES == 0
    zeros = jnp.zeros((h,), F32)
    alog_col = jnp.tile(jnp.concatenate([zeros, a_log.astype(F32)])[:, None], (1, LANES))
    dtb_col = jnp.tile(jnp.concatenate([zeros, dt_bias.astype(F32)])[:, None], (1, LANES))
    r = jnp.arange(rows)
    utri = ((r[:, None] // DN_CHUNK == r[None, :] // DN_CHUNK) & (r[:, None] <= r[None, :])).astype(BF16)
    src = jnp.arange(2 * h)[:, None]
    dst = jnp.arange(2 * dn_w)[None, :]
    expander = ((dst // dn_w == src // h) & ((dst % dn_w) // HEAD_DIM == src % h)).astype(BF16)
    expander = jnp.tile(expander, (3, 1))
    hl = jnp.arange(dn_w) // HEAD_DIM
    hsum = (hl[:, None] == hl[None, :]).astype(BF16)
    const = lambda a: pl.BlockSpec(a.shape, lambda t: (0,) * a.ndim)

    def produced(t):
        g = jnp.minimum(t, nsteps - 1)
        return g // nc, g % nc, 0

    def consumed(t):
        g = jnp.maximum(t - 1, 0)
        return g // nc, g % nc, 0

    est = (2 * (rows * w3 * 2 + 2 * h * rows * 4 + rows * dn_w * 4) + (rows + 8) * w3 * 4 + 12 * rows * w3 * 4
           + 6 * rows * dn_w * 2 + rows * dn_w * 4 + 2 * (utri.size + expander.size + hsum.size) * 2)
    return pl.pallas_call(
        functools.partial(_gdn_kernel, rows=rows, groups_per_seq=nc),
        grid=(nsteps + 1,),
        in_specs=[pl.BlockSpec((1, rows, w3), produced),
                  pl.BlockSpec((2 * h, rows), lambda t: (0, jnp.minimum(t, nsteps - 1))),
                  const(conv_w), const(alog_col), const(dtb_col), const(utri), const(expander), const(hsum)],
        out_specs=pl.BlockSpec((1, rows, dn_w), consumed),
        out_shape=jax.ShapeDtypeStruct((b, s, dn_w), F32),
        scratch_shapes=[pltpu.VMEM((rows + 8, w3), F32), pltpu.VMEM((DN_HEADS // 2, LANES, LANES), F32),
                        pltpu.VMEM((6, rows, dn_w), BF16), pltpu.VMEM((rows, dn_w), F32)],
        compiler_params=pltpu.CompilerParams(dimension_semantics=("arbitrary",),
                                             vmem_limit_bytes=_vmem_limit(est)),
        name="gdn",
    )(qkv_dn, small, conv_w.astype(F32), alog_col, dtb_col, utri, expander, hsum)


def _mixout_kernel(x_ref, ya_ref, od_ref, z_ref, gate_ref, dnw_ref, hmean_ref, wa_ref, wd_ref, wo_ref, pn_ref, o_ref):
    d = x_ref.shape[1]
    od = od_ref[...]
    ms = _dot((od * od).astype(BF16), hmean_ref[...])
    z = z_ref[...].astype(F32)
    y_dn = od * lax.rsqrt(ms + NORM_EPS) * dnw_ref[...] * (z * _sigmoid(z))
    ga = _sigmoid(gate_ref[:, :d].astype(F32))
    gd = _sigmoid(gate_ref[:, d:].astype(F32))
    merged = ga * _dot(ya_ref[...], wa_ref[...]) + gd * _dot(y_dn.astype(BF16), wd_ref[...])
    y = _dot(merged.astype(BF16), wo_ref[...])
    o_ref[...] = x_ref[...] + _rms(y, pn_ref[...])


def _mixout(x2, y_att, o_dn, z, gates, dn_norm, wa, wd, wo, post_norm, *, tm):
    t, d = x2.shape
    dn_w = DN_HEADS * HEAD_DIM
    dnw_row = jnp.tile(dn_norm.astype(F32), DN_HEADS)[None, :]
    hl = jnp.arange(dn_w) // HEAD_DIM
    hmean = ((hl[:, None] == hl[None, :]).astype(F32) / HEAD_DIM).astype(BF16)
    row = lambda w: pl.BlockSpec((tm, w), lambda i: (i, 0))
    const = lambda a: pl.BlockSpec(a.shape, lambda i: (0, 0), pipeline_mode=pl.Buffered(1))
    est = (2 * (2 * tm * d * 4 + tm * dn_w * (2 + 4 + 2) + tm * 2 * d * 2)
           + (hmean.size + wa.size + wd.size + wo.size) * 2 + 8 * tm * d * 4)
    return pl.pallas_call(
        _mixout_kernel,
        grid=(t // tm,),
        in_specs=[row(d), row(dn_w), row(dn_w), row(dn_w), row(2 * d), const(dnw_row), const(hmean),
                  const(wa), const(wd), const(wo), const(post_norm)],
        out_specs=row(d),
        out_shape=jax.ShapeDtypeStruct((t, d), F32),
        compiler_params=pltpu.CompilerParams(dimension_semantics=("arbitrary",), vmem_limit_bytes=_vmem_limit(est)),
        name="mixout",
    )(x2, y_att, o_dn, z, gates, dnw_row, hmean, wa, wd, wo, post_norm)


def _mlp_kernel(x_ref, pre_ref, w1_ref, w2_ref, post_ref, o_ref, *, ff_chunk):
    x = x_ref[...]
    hb = _rms(x, pre_ref[...]).astype(BF16)
    acc = jnp.zeros(x.shape, F32)
    for c in range(0, w1_ref.shape[1], ff_chunk):
        a = jnp.maximum(_dot(hb, w1_ref[:, c:c + ff_chunk]), 0.0)
        acc = acc + _dot((a * a).astype(BF16), w2_ref[c:c + ff_chunk, :])
    o_ref[...] = x + _rms(acc, post_ref[...])


def _mlp(x1, pre, w1, w2, post, *, tm):
    t, d = x1.shape
    row = pl.BlockSpec((tm, d), lambda i: (i, 0))
    const = lambda a: pl.BlockSpec(a.shape, lambda i: (0, 0), pipeline_mode=pl.Buffered(1))
    est = 2 * (2 * tm * d * 4) + (w1.size + w2.size) * 2 + 6 * tm * d * 4 + 2 * tm * 1024 * 4
    return pl.pallas_call(
        functools.partial(_mlp_kernel, ff_chunk=1024),
        grid=(t // tm,),
        in_specs=[row, const(pre), const(w1), const(w2), const(post)],
        out_specs=row,
        out_shape=jax.ShapeDtypeStruct((t, d), F32),
        compiler_params=pltpu.CompilerParams(dimension_semantics=("arbitrary",), vmem_limit_bytes=_vmem_limit(est)),
        name="mlp",
    )(x1, pre, w1, w2, post)


def kernel(x, pre_norm_mix, w_in, conv_w, a_log, dt_bias, dn_norm, w_branch_att, w_branch_dn,
           w_out, post_norm_mix, pre_norm_mlp, w_mlp_in, w_mlp_out, post_norm_mlp):
    b, s, d = x.shape
    att_w = ATT_HEADS * HEAD_DIM
    dn_w = DN_HEADS * HEAD_DIM
    n_main = 3 * att_w + 3 * dn_w + dn_w
    tm = 1024
    x2 = x.reshape(b * s, d)
    for l in range(w_in.shape[0]):
        wl = w_in[l]
        col_scale = jnp.where(jnp.arange(n_main) < att_w, HEAD_DIM ** -0.5 * LOG2_E, 1.0).astype(F32)
        w_main = (wl[:, :n_main] * col_scale[None, :]).astype(BF16)
        w_gate = wl[:, n_main + 2 * DN_HEADS:].astype(BF16)
        w_small = wl[:, n_main:n_main + 2 * DN_HEADS].T.astype(BF16)
        qkv_att, qkv_dn, z, gates, small = _inproj(x2, pre_norm_mix[l][None, :], w_main, w_gate, w_small, tm=tm)
        y_att = _moba(qkv_att.reshape(b, s, 3 * att_w))
        o_dn = _gdn(qkv_dn.reshape(b, s, 3 * dn_w), small, conv_w[l], a_log[l], dt_bias[l], rows=256)
        x2 = _mixout(x2, y_att.reshape(b * s, att_w), o_dn.reshape(b * s, dn_w), z, gates, dn_norm[l],
                     w_branch_att[l].astype(BF16), w_branch_dn[l].astype(BF16), w_out[l].astype(BF16),
                     post_norm_mix[l][None, :], tm=tm)
        x2 = _mlp(x2, pre_norm_mlp[l][None, :], w_mlp_in[l].astype(BF16), w_mlp_out[l].astype(BF16),
                  post_norm_mlp[l][None, :], tm=tm)
    return x2.reshape(b, s, d)
```

```python
import functools

import jax
import jax.numpy as jnp
from jax import lax
from jax.experimental import pallas as pl
from jax.experimental.pallas import tpu as pltpu

ATT_HEADS = 8
DN_HEADS = 8
HEAD_DIM = 64
MOBA_BLOCK = 256
MOBA_TOPK = 3
DN_CHUNK = 64
CONV_WIDTH = 4
NORM_EPS = 1e-6

LANES = 128
V7X_VMEM_BYTES = 64 * 1024 * 1024
VMEM_CAP_BYTES = V7X_VMEM_BYTES - 8 * 1024 * 1024

F32 = jnp.float32
BF16 = jnp.bfloat16
NEG_BIG = -1e30
MASK_BIAS = -(2.0 ** 100)
LOG2_E = 1.4426950408889634
MOBA_GROUP = 4

_NT = (((1,), (1,)), ((), ()))
_TN = (((0,), (0,)), ((), ()))


def _vmem_limit(nbytes):
    return int(min(VMEM_CAP_BYTES, nbytes * 5 // 4 + (4 << 20)))


def _dot(a, b):
    return jnp.dot(a, b, preferred_element_type=F32)


def _rms(x, w):
    return x * lax.rsqrt(jnp.mean(x * x, axis=-1, keepdims=True) + NORM_EPS) * w


def _sigmoid(x):
    return 1.0 / (1.0 + jnp.exp(-x))


def _split3(x):
    hi = x.astype(BF16)
    r = x - hi.astype(F32)
    mid = r.astype(BF16)
    lo = (r - mid.astype(F32)).astype(BF16)
    return hi, mid, lo


def _inproj_kernel(x_ref, g_ref, wm_ref, wg_ref, ws_ref, att_ref, dn_ref, z_ref, gate_ref, small_ref, *, col_chunk):
    hb = _rms(x_ref[...], g_ref[...]).astype(BF16)
    start = 0
    for ref in (att_ref, dn_ref, z_ref):
        width = ref.shape[1]
        for c in range(0, width, col_chunk):
            ref[:, c:c + col_chunk] = _dot(hb, wm_ref[:, start + c:start + c + col_chunk]).astype(ref.dtype)
        start += width
    for c in range(0, gate_ref.shape[1], col_chunk):
        gate_ref[:, c:c + col_chunk] = _dot(hb, wg_ref[:, c:c + col_chunk]).astype(gate_ref.dtype)
    small_ref[...] = lax.dot_general(ws_ref[...], hb, _NT, preferred_element_type=F32)


def _inproj(x2, gain, w_main, w_gate, w_small, *, tm):
    t, d = x2.shape
    att_w, dn_w, z_w, gate_w = 3 * ATT_HEADS * HEAD_DIM, 3 * DN_HEADS * HEAD_DIM, DN_HEADS * HEAD_DIM, 2 * d
    assert w_main.shape == (d, att_w + dn_w + z_w) and w_gate.shape == (d, gate_w) and t % tm == 0
    n_out = w_main.shape[1] + gate_w
    row = lambda w: pl.BlockSpec((tm, w), lambda i: (i, 0))
    const = lambda shp: pl.BlockSpec(shp, lambda i: (0, 0), pipeline_mode=pl.Buffered(1))
    est = (2 * (tm * d * 4 + tm * n_out * 2 + tm * LANES * 4) + (w_main.size + w_gate.size + w_small.size) * 2
           + 3 * tm * d * 4 + 2 * tm * 512 * 4)
    return pl.pallas_call(
        functools.partial(_inproj_kernel, col_chunk=512),
        grid=(t // tm,),
        in_specs=[row(d), const((1, d)), const(w_main.shape), const(w_gate.shape), const(w_small.shape)],
        out_specs=[row(att_w), row(dn_w), row(z_w), row(gate_w), pl.BlockSpec((w_small.shape[0], tm), lambda i: (0, i))],
        out_shape=[jax.ShapeDtypeStruct((t, att_w), BF16), jax.ShapeDtypeStruct((t, dn_w), BF16),
                   jax.ShapeDtypeStruct((t, z_w), BF16), jax.ShapeDtypeStruct((t, gate_w), BF16),
                   jax.ShapeDtypeStruct((w_small.shape[0], t), F32)],
        compiler_params=pltpu.CompilerParams(dimension_semantics=("arbitrary",), vmem_limit_bytes=_vmem_limit(est)),
        name="inproj",
    )(x2, gain, w_main, w_gate, w_small)


def _moba_kernel(q_ref, k_ref, v_ref, o_ref, *, nb):
    blk = MOBA_BLOCK
    lane = lax.broadcasted_iota(jnp.int32, (blk, LANES), 1)
    first_head = lane < HEAD_DIM

    km = jnp.concatenate(
        [jnp.sum(k_ref[0, n * blk:(n + 1) * blk, :].astype(F32), axis=0, keepdims=True) for n in range(nb)]
        + [jnp.zeros((8 - nb, LANES), F32)] * (1 if nb < 8 else 0), axis=0) * (1.0 / blk)
    l8 = lax.broadcasted_iota(jnp.int32, (8, LANES), 1) < HEAD_DIM
    pad = jnp.zeros((HEAD_DIM - 8, LANES), F32)
    g = jnp.concatenate([jnp.where(l8, 0.0, km), pad, jnp.where(l8, km, 0.0), pad], axis=0)
    g_hi = g.astype(BF16)
    g_lo = (g - g_hi.astype(F32)).astype(BF16)

    s_len = nb * blk
    kblk = lax.broadcasted_iota(jnp.int32, (s_len, LANES), 0) // blk
    klane = lax.broadcasted_iota(jnp.int32, (s_len, LANES), 1)
    k2 = k_ref[0]
    k_aug = (jnp.where(klane < HEAD_DIM, k2, jnp.where(klane - HEAD_DIM == kblk, 1.0, 0.0).astype(BF16)),
             jnp.where(klane < HEAD_DIM, jnp.where(klane == kblk, 1.0, 0.0).astype(BF16), k2))

    row = lax.broadcasted_iota(jnp.int32, (8, blk), 0)

    def drop_bias(s, j):
        out = jnp.zeros((8, blk), F32)
        past = row < j
        for n in range(j):
            rn = s[n:n + 1, :]
            beats = jnp.where(s > rn, 1.0, jnp.where(jnp.logical_and(s == rn, row < n), 1.0, 0.0))
            cnt = jnp.sum(jnp.where(past, beats, 0.0), axis=0, keepdims=True)
            out = jnp.where(row == n, jnp.where(cnt < float(MOBA_TOPK), 0.0, MASK_BIAS), out)
        return out

    def queries(j):
        q2 = q_ref[0, j * blk:(j + 1) * blk, :]
        qs = q2
        zero = jnp.zeros_like(qs)
        if j <= MOBA_TOPK:
            return jnp.where(first_head, qs, zero), jnp.where(first_head, zero, qs)
        s_t = (lax.dot_general(g_hi, q2, _NT, preferred_element_type=F32)
               + lax.dot_general(g_lo, q2, _NT, preferred_element_type=F32))
        padq = jnp.zeros((HEAD_DIM - 8, blk), F32)
        bias_t = jnp.concatenate([drop_bias(s_t[0:8], j), padq, drop_bias(s_t[HEAD_DIM:HEAD_DIM + 8], j), padq], axis=0)
        bias = bias_t.T.astype(BF16)
        return jnp.where(first_head, qs, bias), jnp.where(first_head, bias, qs)

    def scores(tiles):
        j0, j1 = tiles[0], tiles[-1]
        qs = {j: queries(j) for j in tiles}
        out = []
        for hd in range(2):
            stack = lambda js: qs[js[0]][hd] if len(js) == 1 else jnp.concatenate([qs[j][hd] for j in js], axis=0)
            big = lax.dot_general(stack(tiles), k_aug[hd][0:(j0 + 1) * blk], _NT, preferred_element_type=F32)
            late = {n: lax.dot_general(stack(list(range(n, j1 + 1))), k_aug[hd][n * blk:(n + 1) * blk], _NT,
                                       preferred_element_type=F32) for n in range(j0 + 1, j1 + 1)}
            out.append((big, late))
        return out

    tri = lax.broadcasted_iota(jnp.int32, (blk, blk), 1) <= lax.broadcasted_iota(jnp.int32, (blk, blk), 0)

    def softmax_terms(parts):
        m = jnp.max(parts[0], axis=1, keepdims=True)
        for pt in parts[1:]:
            m = jnp.maximum(m, jnp.max(pt, axis=1, keepdims=True))
        return [jnp.exp2(pt - m).astype(BF16) for pt in parts]

    def normalised(pv):
        return pv / pltpu.roll(pv, HEAD_DIM, axis=1)

    v2 = v_ref[0]
    ones = jnp.ones_like(v2)
    v_aug = (jnp.where(klane < HEAD_DIM, v2, ones), jnp.where(klane < HEAD_DIM, ones, v2))
    groups = [list(range(j0, min(j0 + MOBA_GROUP, nb))) for j0 in range(0, nb, MOBA_GROUP)]
    sc = scores(groups[0])
    for gi, tiles in enumerate(groups):
        nxt = scores(groups[gi + 1]) if gi + 1 < len(groups) else None
        j0, j1 = tiles[0], tiles[-1]
        w0 = (j0 + 1) * blk
        outs = {j: [] for j in tiles}
        for hd in range(2):
            big, late = sc[hd]
            probs = {}
            for j in tiles:
                rows = slice((j - j0) * blk, (j - j0 + 1) * blk)
                parts = [big[rows, n * blk:(n + 1) * blk] for n in range(j0 + 1)]
                parts += [late[n][(j - n) * blk:(j - n + 1) * blk, :] for n in range(j0 + 1, j + 1)]
                parts[-1] = jnp.where(tri, parts[-1], NEG_BIG)
                probs[j] = softmax_terms(parts)
            cat = lambda xs, axis: xs[0] if len(xs) == 1 else jnp.concatenate(xs, axis=axis)
            pv_big = _dot(cat([cat(probs[j][:j0 + 1], 1) for j in tiles], 0), v_aug[hd][0:w0])
            pv = {j: pv_big[(j - j0) * blk:(j - j0 + 1) * blk] for j in tiles}
            for n in range(j0 + 1, j1 + 1):
                js = list(range(n, j1 + 1))
                pv_n = _dot(cat([probs[j][n] for j in js], 0), v_aug[hd][n * blk:(n + 1) * blk])
                for j in js:
                    pv[j] = pv[j] + pv_n[(j - n) * blk:(j - n + 1) * blk]
            for j in tiles:
                outs[j].append(normalised(pv[j]))
        for j in tiles:
            o_ref[0, j * blk:(j + 1) * blk, :] = jnp.where(first_head, outs[j][0], outs[j][1]).astype(o_ref.dtype)
        sc = nxt


def _moba(qkv):
    b, s, w3 = qkv.shape
    blk = MOBA_BLOCK
    npair = ATT_HEADS // 2
    assert w3 == 3 * ATT_HEADS * HEAD_DIM and s % blk == 0 and s // blk <= 8
    nb = s // blk
    est = 2 * 4 * s * LANES * 2 + 16 * blk * s * (4 + 4 + 2)
    return pl.pallas_call(
        functools.partial(_moba_kernel, nb=nb),
        grid=(b, npair),
        in_specs=[pl.BlockSpec((1, s, LANES), lambda i, p: (i, 0, p)),
                  pl.BlockSpec((1, s, LANES), lambda i, p: (i, 0, npair + p)),
                  pl.BlockSpec((1, s, LANES), lambda i, p: (i, 0, 2 * npair + p))],
        out_specs=pl.BlockSpec((1, s, LANES), lambda i, p: (i, 0, p)),
        out_shape=jax.ShapeDtypeStruct((b, s, ATT_HEADS * HEAD_DIM), BF16),
        compiler_params=pltpu.CompilerParams(dimension_semantics=("arbitrary", "arbitrary"),
                                             vmem_limit_bytes=_vmem_limit(est)),
        name="moba",
    )(qkv, qkv, qkv)


_STG_KB, _STG_QN, _STG_KN, _STG_VB, _STG_KBE, _STG_QD = range(6)


def _gdn_kernel(x_ref, sm_ref, cw_ref, alog_ref, dtb_ref, utri_ref, exp_ref, hsum_ref, o_ref,
                xs_ref, st_ref, stg_ref, gx_ref, *, rows, groups_per_seq):
    ck = DN_CHUNK
    dn_w = DN_HEADS * HEAD_DIM
    npair = DN_HEADS // 2
    t = pl.program_id(0)
    halo = 8

    @pl.when(t == 0)
    def _():
        xs_ref[0:halo, :] = jnp.zeros((halo, 3 * dn_w), F32)
        st_ref[...] = jnp.zeros_like(st_ref)
        stg_ref[...] = jnp.zeros_like(stg_ref)
        gx_ref[...] = jnp.zeros_like(gx_ref)

    ri = lax.broadcasted_iota(jnp.int32, (ck, dn_w), 0)
    ci = jnp.bitwise_and(lax.broadcasted_iota(jnp.int32, (ck, dn_w), 1), ck - 1)
    tri, strict, diag = ri >= ci, ri > ci, ri == ci
    r2 = lax.broadcasted_iota(jnp.int32, (LANES, LANES), 0) // HEAD_DIM
    c2 = lax.broadcasted_iota(jnp.int32, (LANES, LANES), 1) // HEAD_DIM
    bmask = r2 == c2
    eye2 = jnp.where(diag[:, :LANES], 1.0, 0.0)

    def bd2(a):
        a = a.astype(BF16)
        return jnp.where(bmask, jnp.concatenate([a, a], axis=0), jnp.zeros((LANES, LANES), BF16))

    nchunk = rows // ck
    chains = [(cc, p) for cc in range(nchunk) for p in range(npair)]
    rs = lambda cc: slice(cc * ck, (cc + 1) * ck)
    ls = lambda p: slice(p * LANES, (p + 1) * LANES)
    staged = lambda which, cc, p: stg_ref[which, rs(cc), ls(p)]

    gch = [gx_ref[rs(cc), :] for cc in range(nchunk)]
    glast = [g_[ck - 1:ck, :] for g_ in gch]
    dec, kdec, gl_exp = [], [], []
    for cc, g_ in enumerate(gch):
        grow = jnp.sum(jnp.where(diag, g_, 0.0), axis=0, keepdims=True)
        dec.append(jnp.where(tri, jnp.exp(jnp.where(tri, g_ - grow, 0.0)), 0.0))
        kdec.append((stg_ref[_STG_KN, rs(cc), :].astype(F32) * jnp.exp(glast[cc] - g_)).astype(BF16))
        gl_exp.append(jnp.exp(glast[cc]))
    qd_old = [stg_ref[_STG_QD, rs(cc), :] for cc in range(nchunk)]

    seq_start = lax.rem(t, groups_per_seq) == 0
    cw = cw_ref[...]
    hsum2 = hsum_ref[0:LANES, 0:LANES]
    act, sumsq = {}, {}

    def prepare(slab):
        cs = ls(slab)
        xs_ref[0:halo, cs] = jnp.where(seq_start, 0.0, xs_ref[0:halo, cs])
        xs_ref[halo:halo + rows, cs] = x_ref[0, :, cs].astype(F32)
        xf = xs_ref[:, cs]
        y = xf * cw[CONV_WIDTH - 1:CONV_WIDTH, cs]
        for j in range(1, CONV_WIDTH):
            y = y + pltpu.roll(xf, j, axis=0) * cw[CONV_WIDTH - 1 - j:CONV_WIDTH - j, cs]
        y = y[halo:]
        xs_ref[0:halo, cs] = xf[rows:]
        y = y * _sigmoid(y)
        act[slab] = y
        if slab < 2 * npair:
            sumsq[slab] = _dot((y * y).astype(BF16), hsum2)

    def stage(p):
        qn = act[p] * (lax.rsqrt(sumsq[p] + NORM_EPS) * (HEAD_DIM ** -0.5))
        kn = act[npair + p] * lax.rsqrt(sumsq[npair + p] + NORM_EPS)
        bx, eg = bexp[:, ls(p)], jnp.exp(gexp[:, ls(p)])
        kb = kn * bx
        stg_ref[_STG_KB, :, ls(p)] = kb.astype(BF16)
        stg_ref[_STG_QN, :, ls(p)] = qn.astype(BF16)
        stg_ref[_STG_KN, :, ls(p)] = kn.astype(BF16)
        stg_ref[_STG_VB, :, ls(p)] = (act[2 * npair + p] * bx).astype(BF16)
        stg_ref[_STG_KBE, :, ls(p)] = (kb * eg).astype(BF16)
        stg_ref[_STG_QD, :, ls(p)] = (qn * eg).astype(BF16)

    todo = list(range(3 * npair))

    def prepare_some(n):
        for _ in range(n):
            if todo:
                prepare(todo.pop(0))

    kq = {(cc, p): lax.dot_general(jnp.concatenate([staged(_STG_KB, cc, p), staged(_STG_QN, cc, p)], axis=0),
                                   bd2(staged(_STG_KN, cc, p)), _NT, preferred_element_type=F32)
          for cc, p in chains}
    vbd = {(cc, p): jnp.concatenate([bd2(staged(_STG_VB, cc, p)), bd2(staged(_STG_KBE, cc, p))], axis=1)
           for cc, p in chains}
    sm = sm_ref[...]
    widen = lambda a: jnp.concatenate([a] * (rows // LANES), axis=1)
    xa = sm + widen(dtb_ref[...])
    softplus = jnp.maximum(xa, 0.0) + jnp.log(1.0 + jnp.exp(-jnp.abs(xa)))
    g = -jnp.exp(widen(alog_ref[...])) * softplus
    utri = utri_ref[...]
    gc = sum(_dot(t_, utri) for t_ in _split3(g))
    prepare_some(2)
    neg_l = {(cc, p): -jnp.where(strict[:, ls(p)], kq[cc, p][:ck] * dec[cc][:, ls(p)], 0.0) for cc, p in chains}
    qk = {(cc, p): jnp.where(tri[:, ls(p)], kq[cc, p][ck:] * dec[cc][:, ls(p)], 0.0) for cc, p in chains}
    ssum = {ch: eye2 + neg_l[ch] for ch in chains}
    pw = {ch: _dot(neg_l[ch].astype(BF16), bd2(neg_l[ch])) for ch in chains}
    gate_row = lax.broadcasted_iota(jnp.int32, (2 * DN_HEADS, rows), 0)
    comb = jnp.where(gate_row < DN_HEADS, _sigmoid(sm), gc)
    ex = lax.dot_general(jnp.concatenate(_split3(comb), axis=0), exp_ref[...], _TN, preferred_element_type=F32)
    bexp, gexp = ex[:, :dn_w], ex[:, dn_w:]
    prepare_some(2)
    span = 2
    while span * 2 < ck:
        both = {ch: _dot(jnp.concatenate([ssum[ch], pw[ch]], axis=0).astype(BF16), bd2(pw[ch])) for ch in chains}
        prepare_some(2)
        ssum = {ch: ssum[ch] + both[ch][:ck] for ch in chains}
        pw = {ch: both[ch][ck:] for ch in chains}
        span *= 2
    corr = {ch: _dot(ssum[ch].astype(BF16), bd2(pw[ch])) for ch in chains}
    prepare_some(len(todo))
    uw = {ch: _dot((ssum[ch] + corr[ch]).astype(BF16), vbd[ch]) for ch in chains}

    fresh = lax.rem(jnp.maximum(t - 1, 0), groups_per_seq) == 0
    state = [jnp.where(fresh, 0.0, st_ref[p]) for p in range(npair)]
    for cc in range(nchunk):
        wq = [_dot(jnp.concatenate([uw[cc, p][:, LANES:].astype(BF16), qd_old[cc][:, ls(p)]], axis=0),
                   state[p].astype(BF16)) for p in range(npair)]
        if cc < npair:
            stage(cc)
        vnew = [uw[cc, p][:, :LANES] - wq[p][:ck] for p in range(npair)]
        intra = [_dot(qk[cc, p].astype(BF16), bd2(vnew[p])) for p in range(npair)]
        upd = [lax.dot_general(kdec[cc][:, ls(p)], vnew[p].astype(BF16), _TN, preferred_element_type=F32)
               for p in range(npair)]
        for p in range(npair):
            o_ref[0, rs(cc), ls(p)] = (wq[p][ck:] + intra[p]).astype(o_ref.dtype)
        state = [state[p] * gl_exp[cc][:, ls(p)] + jnp.where(bmask, upd[p], 0.0) for p in range(npair)]
    for p in range(npair):
        st_ref[p] = state[p]
    for p in range(nchunk, npair):
        stage(p)
    gx_ref[...] = gexp


def _gdn(qkv_dn, small, conv_w, a_log, dt_bias, *, rows):
    b, s, w3 = qkv_dn.shape
    dn_w = DN_HEADS * HEAD_DIM
    assert w3 == 3 * dn_w and s % rows == 0 and rows % DN_CHUNK == 0
    h = DN_HEADS
    nc = s // rows
    nsteps = b * nc
    assert small.shape == (2 * h, b * s) and rows % LANES == 0
    zeros = jnp.zeros((h,), F32)
    alog_col = jnp.tile(jnp.concatenate([zeros, a_log.astype(F32)])[:, None], (1, LANES))
    dtb_col = jnp.tile(jnp.concatenate([zeros, dt_bias.astype(F32)])[:, None], (1, LANES))
    r = jnp.arange(rows)
    utri = ((r[:, None] // DN_CHUNK == r[None, :] // DN_CHUNK) & (r[:, None] <= r[None, :])).astype(BF16)
    src = jnp.arange(2 * h)[:, None]
    dst = jnp.arange(2 * dn_w)[None, :]
    expander = ((dst // dn_w == src // h) & ((dst % dn_w) // HEAD_DIM == src % h)).astype(BF16)
    expander = jnp.tile(expander, (3, 1))
    hl = jnp.arange(dn_w) // HEAD_DIM
    hsum = (hl[:, None] == hl[None, :]).astype(BF16)
    const = lambda a: pl.BlockSpec(a.shape, lambda t: (0,) * a.ndim)

    def produced(t):
        g = jnp.minimum(t, nsteps - 1)
        return g // nc, g % nc, 0

    def consumed(t):
        g = jnp.maximum(t - 1, 0)
        return g // nc, g % nc, 0

    est = (2 * (rows * w3 * 2 + 2 * h * rows * 4 + rows * dn_w * 4) + (rows + 8) * w3 * 4 + 12 * rows * w3 * 4
           + 6 * rows * dn_w * 2 + rows * dn_w * 4 + 2 * (utri.size + expander.size + hsum.size) * 2)
    return pl.pallas_call(
        functools.partial(_gdn_kernel, rows=rows, groups_per_seq=nc),
        grid=(nsteps + 1,),
        in_specs=[pl.BlockSpec((1, rows, w3), produced),
                  pl.BlockSpec((2 * h, rows), lambda t: (0, jnp.minimum(t, nsteps - 1))),
                  const(conv_w), const(alog_col), const(dtb_col), const(utri), const(expander), const(hsum)],
        out_specs=pl.BlockSpec((1, rows, dn_w), consumed),
        out_shape=jax.ShapeDtypeStruct((b, s, dn_w), F32),
        scratch_shapes=[pltpu.VMEM((rows + 8, w3), F32), pltpu.VMEM((DN_HEADS // 2, LANES, LANES), F32),
                        pltpu.VMEM((6, rows, dn_w), BF16), pltpu.VMEM((rows, dn_w), F32)],
        compiler_params=pltpu.CompilerParams(dimension_semantics=("arbitrary",),
                                             vmem_limit_bytes=_vmem_limit(est)),
        name="gdn",
    )(qkv_dn, small, conv_w.astype(F32), alog_col, dtb_col, utri, expander, hsum)


def _mixout_kernel(x_ref, ya_ref, od_ref, z_ref, gate_ref, dnw_ref, hmean_ref, wa_ref, wd_ref, wo_ref, pn_ref, o_ref):
    d = x_ref.shape[1]
    od = od_ref[...]
    ms = _dot((od * od).astype(BF16), hmean_ref[...])
    z = z_ref[...].astype(F32)
    y_dn = od * lax.rsqrt(ms + NORM_EPS) * dnw_ref[...] * (z * _sigmoid(z))
    ga = _sigmoid(gate_ref[:, :d].astype(F32))
    gd = _sigmoid(gate_ref[:, d:].astype(F32))
    merged = ga * _dot(ya_ref[...], wa_ref[...]) + gd * _dot(y_dn.astype(BF16), wd_ref[...])
    y = _dot(merged.astype(BF16), wo_ref[...])
    o_ref[...] = x_ref[...] + _rms(y, pn_ref[...])


def _mixout(x2, y_att, o_dn, z, gates, dn_norm, wa, wd, wo, post_norm, *, tm):
    t, d = x2.shape
    dn_w = DN_HEADS * HEAD_DIM
    dnw_row = jnp.tile(dn_norm.astype(F32), DN_HEADS)[None, :]
    hl = jnp.arange(dn_w) // HEAD_DIM
    hmean = ((hl[:, None] == hl[None, :]).astype(F32) / HEAD_DIM).astype(BF16)
    row = lambda w: pl.BlockSpec((tm, w), lambda i: (i, 0))
    const = lambda a: pl.BlockSpec(a.shape, lambda i: (0, 0), pipeline_mode=pl.Buffered(1))
    est = (2 * (2 * tm * d * 4 + tm * dn_w * (2 + 4 + 2) + tm * 2 * d * 2)
           + (hmean.size + wa.size + wd.size + wo.size) * 2 + 8 * tm * d * 4)
    return pl.pallas_call(
        _mixout_kernel,
        grid=(t // tm,),
        in_specs=[row(d), row(dn_w), row(dn_w), row(dn_w), row(2 * d), const(dnw_row), const(hmean),
                  const(wa), const(wd), const(wo), const(post_norm)],
        out_specs=row(d),
        out_shape=jax.ShapeDtypeStruct((t, d), F32),
        compiler_params=pltpu.CompilerParams(dimension_semantics=("arbitrary",), vmem_limit_bytes=_vmem_limit(est)),
        name="mixout",
    )(x2, y_att, o_dn, z, gates, dnw_row, hmean, wa, wd, wo, post_norm)


def _mlp_kernel(x_ref, pre_ref, w1_ref, w2_ref, post_ref, o_ref, *, ff_chunk):
    x = x_ref[...]
    hb = _rms(x, pre_ref[...]).astype(BF16)
    acc = jnp.zeros(x.shape, F32)
    for c in range(0, w1_ref.shape[1], ff_chunk):
        a = jnp.maximum(_dot(hb, w1_ref[:, c:c + ff_chunk]), 0.0)
        acc = acc + _dot((a * a).astype(BF16), w2_ref[c:c + ff_chunk, :])
    o_ref[...] = x + _rms(acc, post_ref[...])


def _mlp(x1, pre, w1, w2, post, *, tm):
    t, d = x1.shape
    row = pl.BlockSpec((tm, d), lambda i: (i, 0))
    const = lambda a: pl.BlockSpec(a.shape, lambda i: (0, 0), pipeline_mode=pl.Buffered(1))
    est = 2 * (2 * tm * d * 4) + (w1.size + w2.size) * 2 + 6 * tm * d * 4 + 2 * tm * 1024 * 4
    return pl.pallas_call(
        functools.partial(_mlp_kernel, ff_chunk=1024),
        grid=(t // tm,),
        in_specs=[row, const(pre), const(w1), const(w2), const(post)],
        out_specs=row,
        out_shape=jax.ShapeDtypeStruct((t, d), F32),
        compiler_params=pltpu.CompilerParams(dimension_semantics=("arbitrary",), vmem_limit_bytes=_vmem_limit(est)),
        name="mlp",
    )(x1, pre, w1, w2, post)


def kernel(x, pre_norm_mix, w_in, conv_w, a_log, dt_bias, dn_norm, w_branch_att, w_branch_dn,
           w_out, post_norm_mix, pre_norm_mlp, w_mlp_in, w_mlp_out, post_norm_mlp):
    b, s, d = x.shape
    att_w = ATT_HEADS * HEAD_DIM
    dn_w = DN_HEADS * HEAD_DIM
    n_main = 3 * att_w + 3 * dn_w + dn_w
    tm = 1024
    x2 = x.reshape(b * s, d)
    for l in range(w_in.shape[0]):
        wl = w_in[l]
        col_scale = jnp.where(jnp.arange(n_main) < att_w, HEAD_DIM ** -0.5 * LOG2_E, 1.0).astype(F32)
        w_main = (wl[:, :n_main] * col_scale[None, :]).astype(BF16)
        w_gate = wl[:, n_main + 2 * DN_HEADS:].astype(BF16)
        w_small = wl[:, n_main:n_main + 2 * DN_HEADS].T.astype(BF16)
        qkv_att, qkv_dn, z, gates, small = _inproj(x2, pre_norm_mix[l][None, :], w_main, w_gate, w_small, tm=tm)
        y_att = _moba(qkv_att.reshape(b, s, 3 * att_w))
        o_dn = _gdn(qkv_dn.reshape(b, s, 3 * dn_w), small, conv_w[l], a_log[l], dt_bias[l], rows=256)
        x2 = _mixout(x2, y_att.reshape(b * s, att_w), o_dn.reshape(b * s, dn_w), z, gates, dn_norm[l],
                     w_branch_att[l].astype(BF16), w_branch_dn[l].astype(BF16), w_out[l].astype(BF16),
                     post_norm_mix[l][None, :], tm=tm)
        x2 = _mlp(x2, pre_norm_mlp[l][None, :], w_mlp_in[l].astype(BF16), w_mlp_out[l].astype(BF16),
                  post_norm_mlp[l][None, :], tm=tm)
    return x2.reshape(b, s, d)
```

```python
import functools

import jax
import jax.numpy as jnp
from jax import lax
from jax.experimental import pallas as pl
from jax.experimental.pallas import tpu as pltpu

ATT_HEADS = 8
DN_HEADS = 8
HEAD_DIM = 64
MOBA_BLOCK = 256
MOBA_TOPK = 3
DN_CHUNK = 64
CONV_WIDTH = 4
NORM_EPS = 1e-6

LANES = 128
V7X_VMEM_BYTES = 64 * 1024 * 1024
VMEM_CAP_BYTES = V7X_VMEM_BYTES - 8 * 1024 * 1024

F32 = jnp.float32
BF16 = jnp.bfloat16
NEG_BIG = -1e30
MASK_BIAS = -(2.0 ** 100)
LOG2_E = 1.4426950408889634
MOBA_GROUP = 8

_NT = (((1,), (1,)), ((), ()))
_TN = (((0,), (0,)), ((), ()))


def _vmem_limit(nbytes):
    return int(min(VMEM_CAP_BYTES, nbytes * 5 // 4 + (4 << 20)))


def _dot(a, b):
    return jnp.dot(a, b, preferred_element_type=F32)


def _rms(x, w):
    return x * lax.rsqrt(jnp.mean(x * x, axis=-1, keepdims=True) + NORM_EPS) * w


def _sigmoid(x):
    return 1.0 / (1.0 + jnp.exp(-x))


def _split3(x):
    hi = x.astype(BF16)
    r = x - hi.astype(F32)
    mid = r.astype(BF16)
    lo = (r - mid.astype(F32)).astype(BF16)
    return hi, mid, lo


def _inproj_kernel(x_ref, g_ref, wm_ref, wg_ref, ws_ref, att_ref, dn_ref, z_ref, gate_ref, small_ref, *, col_chunk):
    hb = _rms(x_ref[...], g_ref[...]).astype(BF16)
    start = 0
    for ref in (att_ref, dn_ref, z_ref):
        width = ref.shape[1]
        for c in range(0, width, col_chunk):
            ref[:, c:c + col_chunk] = _dot(hb, wm_ref[:, start + c:start + c + col_chunk]).astype(ref.dtype)
        start += width
    for c in range(0, gate_ref.shape[1], col_chunk):
        gate_ref[:, c:c + col_chunk] = _dot(hb, wg_ref[:, c:c + col_chunk]).astype(gate_ref.dtype)
    small_ref[...] = lax.dot_general(ws_ref[...], hb, _NT, preferred_element_type=F32)


def _inproj(x2, gain, w_main, w_gate, w_small, *, tm):
    t, d = x2.shape
    att_w, dn_w, z_w, gate_w = 3 * ATT_HEADS * HEAD_DIM, 3 * DN_HEADS * HEAD_DIM, DN_HEADS * HEAD_DIM, 2 * d
    assert w_main.shape == (d, att_w + dn_w + z_w) and w_gate.shape == (d, gate_w) and t % tm == 0
    n_out = w_main.shape[1] + gate_w
    row = lambda w: pl.BlockSpec((tm, w), lambda i: (i, 0))
    const = lambda shp: pl.BlockSpec(shp, lambda i: (0, 0), pipeline_mode=pl.Buffered(1))
    est = (2 * (tm * d * 4 + tm * n_out * 2 + tm * LANES * 4) + (w_main.size + w_gate.size + w_small.size) * 2
           + 3 * tm * d * 4 + 2 * tm * 512 * 4)
    return pl.pallas_call(
        functools.partial(_inproj_kernel, col_chunk=512),
        grid=(t // tm,),
        in_specs=[row(d), const((1, d)), const(w_main.shape), const(w_gate.shape), const(w_small.shape)],
        out_specs=[row(att_w), row(dn_w), row(z_w), row(gate_w), pl.BlockSpec((w_small.shape[0], tm), lambda i: (0, i))],
        out_shape=[jax.ShapeDtypeStruct((t, att_w), BF16), jax.ShapeDtypeStruct((t, dn_w), BF16),
                   jax.ShapeDtypeStruct((t, z_w), BF16), jax.ShapeDtypeStruct((t, gate_w), BF16),
                   jax.ShapeDtypeStruct((w_small.shape[0], t), F32)],
        compiler_params=pltpu.CompilerParams(dimension_semantics=("arbitrary",), vmem_limit_bytes=_vmem_limit(est)),
        name="inproj",
    )(x2, gain, w_main, w_gate, w_small)


def _moba_kernel(q_ref, k_ref, v_ref, o_ref, *, nb):
    blk = MOBA_BLOCK
    lane = lax.broadcasted_iota(jnp.int32, (blk, LANES), 1)
    first_head = lane < HEAD_DIM

    km = jnp.concatenate(
        [jnp.sum(k_ref[0, n * blk:(n + 1) * blk, :].astype(F32), axis=0, keepdims=True) for n in range(nb)]
        + [jnp.zeros((8 - nb, LANES), F32)] * (1 if nb < 8 else 0), axis=0) * (1.0 / blk)
    l8 = lax.broadcasted_iota(jnp.int32, (8, LANES), 1) < HEAD_DIM
    pad = jnp.zeros((HEAD_DIM - 8, LANES), F32)
    g = jnp.concatenate([jnp.where(l8, 0.0, km), pad, jnp.where(l8, km, 0.0), pad], axis=0)
    g_hi = g.astype(BF16)
    g_lo = (g - g_hi.astype(F32)).astype(BF16)

    s_len = nb * blk
    kblk = lax.broadcasted_iota(jnp.int32, (s_len, LANES), 0) // blk
    klane = lax.broadcasted_iota(jnp.int32, (s_len, LANES), 1)
    k2 = k_ref[0]
    k_aug = (jnp.where(klane < HEAD_DIM, k2, jnp.where(klane - HEAD_DIM == kblk, 1.0, 0.0).astype(BF16)),
             jnp.where(klane < HEAD_DIM, jnp.where(klane == kblk, 1.0, 0.0).astype(BF16), k2))

    row = lax.broadcasted_iota(jnp.int32, (8, blk), 0)

    def drop_bias(s, j):
        out = jnp.zeros((8, blk), F32)
        past = row < j
        for n in range(j):
            rn = s[n:n + 1, :]
            beats = jnp.where(s > rn, 1.0, jnp.where(jnp.logical_and(s == rn, row < n), 1.0, 0.0))
            cnt = jnp.sum(jnp.where(past, beats, 0.0), axis=0, keepdims=True)
            out = jnp.where(row == n, jnp.where(cnt < float(MOBA_TOPK), 0.0, MASK_BIAS), out)
        return out

    def queries(j):
        q2 = q_ref[0, j * blk:(j + 1) * blk, :]
        qs = q2
        zero = jnp.zeros_like(qs)
        if j <= MOBA_TOPK:
            return jnp.where(first_head, qs, zero), jnp.where(first_head, zero, qs)
        s_t = (lax.dot_general(g_hi, q2, _NT, preferred_element_type=F32)
               + lax.dot_general(g_lo, q2, _NT, preferred_element_type=F32))
        padq = jnp.zeros((HEAD_DIM - 8, blk), F32)
        bias_t = jnp.concatenate([drop_bias(s_t[0:8], j), padq, drop_bias(s_t[HEAD_DIM:HEAD_DIM + 8], j), padq], axis=0)
        bias = bias_t.T.astype(BF16)
        return jnp.where(first_head, qs, bias), jnp.where(first_head, bias, qs)

    def scores(tiles):
        j0, j1 = tiles[0], tiles[-1]
        qs = {j: queries(j) for j in tiles}
        out = []
        for hd in range(2):
            stack = lambda js: qs[js[0]][hd] if len(js) == 1 else jnp.concatenate([qs[j][hd] for j in js], axis=0)
            big = lax.dot_general(stack(tiles), k_aug[hd][0:(j0 + 1) * blk], _NT, preferred_element_type=F32)
            late = {n: lax.dot_general(stack(list(range(n, j1 + 1))), k_aug[hd][n * blk:(n + 1) * blk], _NT,
                                       preferred_element_type=F32) for n in range(j0 + 1, j1 + 1)}
            out.append((big, late))
        return out

    tri = lax.broadcasted_iota(jnp.int32, (blk, blk), 1) <= lax.broadcasted_iota(jnp.int32, (blk, blk), 0)

    def softmax_terms(parts):
        m = jnp.max(parts[0], axis=1, keepdims=True)
        for pt in parts[1:]:
            m = jnp.maximum(m, jnp.max(pt, axis=1, keepdims=True))
        return [jnp.exp2(pt - m).astype(BF16) for pt in parts]

    def normalised(pv):
        return pv / pltpu.roll(pv, HEAD_DIM, axis=1)

    v2 = v_ref[0]
    ones = jnp.ones_like(v2)
    v_aug = (jnp.where(klane < HEAD_DIM, v2, ones), jnp.where(klane < HEAD_DIM, ones, v2))
    groups = [list(range(j0, min(j0 + MOBA_GROUP, nb))) for j0 in range(0, nb, MOBA_GROUP)]
    sc = scores(groups[0])
    for gi, tiles in enumerate(groups):
        nxt = scores(groups[gi + 1]) if gi + 1 < len(groups) else None
        j0, j1 = tiles[0], tiles[-1]
        w0 = (j0 + 1) * blk
        outs = {j: [] for j in tiles}
        for hd in range(2):
            big, late = sc[hd]
            probs = {}
            for j in tiles:
                rows = slice((j - j0) * blk, (j - j0 + 1) * blk)
                parts = [big[rows, n * blk:(n + 1) * blk] for n in range(j0 + 1)]
                parts += [late[n][(j - n) * blk:(j - n + 1) * blk, :] for n in range(j0 + 1, j + 1)]
                parts[-1] = jnp.where(tri, parts[-1], NEG_BIG)
                probs[j] = softmax_terms(parts)
            cat = lambda xs, axis: xs[0] if len(xs) == 1 else jnp.concatenate(xs, axis=axis)
            pv_big = _dot(cat([cat(probs[j][:j0 + 1], 1) for j in tiles], 0), v_aug[hd][0:w0])
            pv = {j: pv_big[(j - j0) * blk:(j - j0 + 1) * blk] for j in tiles}
            for n in range(j0 + 1, j1 + 1):
                js = list(range(n, j1 + 1))
                pv_n = _dot(cat([probs[j][n] for j in js], 0), v_aug[hd][n * blk:(n + 1) * blk])
                for j in js:
                    pv[j] = pv[j] + pv_n[(j - n) * blk:(j - n + 1) * blk]
            for j in tiles:
                outs[j].append(normalised(pv[j]))
        for j in tiles:
            o_ref[0, j * blk:(j + 1) * blk, :] = jnp.where(first_head, outs[j][0], outs[j][1]).astype(o_ref.dtype)
        sc = nxt


def _moba(qkv):
    b, s, w3 = qkv.shape
    blk = MOBA_BLOCK
    npair = ATT_HEADS // 2
    assert w3 == 3 * ATT_HEADS * HEAD_DIM and s % blk == 0 and s // blk <= 8
    nb = s // blk
    est = 2 * 4 * s * LANES * 2 + 16 * blk * s * (4 + 4 + 2)
    return pl.pallas_call(
        functools.partial(_moba_kernel, nb=nb),
        grid=(b, npair),
        in_specs=[pl.BlockSpec((1, s, LANES), lambda i, p: (i, 0, p)),
                  pl.BlockSpec((1, s, LANES), lambda i, p: (i, 0, npair + p)),
                  pl.BlockSpec((1, s, LANES), lambda i, p: (i, 0, 2 * npair + p))],
        out_specs=pl.BlockSpec((1, s, LANES), lambda i, p: (i, 0, p)),
        out_shape=jax.ShapeDtypeStruct((b, s, ATT_HEADS * HEAD_DIM), BF16),
        compiler_params=pltpu.CompilerParams(dimension_semantics=("arbitrary", "arbitrary"),
                                             vmem_limit_bytes=_vmem_limit(est)),
        name="moba",
    )(qkv, qkv, qkv)


_STG_KB, _STG_QN, _STG_KN, _STG_VB, _STG_KBE, _STG_QD = range(6)


def _gdn_kernel(x_ref, sm_ref, cw_ref, alog_ref, dtb_ref, utri_ref, exp_ref, hsum_ref, o_ref,
                xs_ref, st_ref, stg_ref, gx_ref, *, rows, groups_per_seq):
    ck = DN_CHUNK
    dn_w = DN_HEADS * HEAD_DIM
    npair = DN_HEADS // 2
    t = pl.program_id(0)
    halo = 8

    @pl.when(t == 0)
    def _():
        xs_ref[0:halo, :] = jnp.zeros((halo, 3 * dn_w), F32)
        st_ref[...] = jnp.zeros_like(st_ref)
        stg_ref[...] = jnp.zeros_like(stg_ref)
        gx_ref[...] = jnp.zeros_like(gx_ref)

    ri = lax.broadcasted_iota(jnp.int32, (ck, dn_w), 0)
    ci = jnp.bitwise_and(lax.broadcasted_iota(jnp.int32, (ck, dn_w), 1), ck - 1)
    tri, strict, diag = ri >= ci, ri > ci, ri == ci
    r2 = lax.broadcasted_iota(jnp.int32, (LANES, LANES), 0) // HEAD_DIM
    c2 = lax.broadcasted_iota(jnp.int32, (LANES, LANES), 1) // HEAD_DIM
    bmask = r2 == c2
    eye2 = jnp.where(diag[:, :LANES], 1.0, 0.0)

    def bd2(a):
        a = a.astype(BF16)
        return jnp.where(bmask, jnp.concatenate([a, a], axis=0), jnp.zeros((LANES, LANES), BF16))

    nchunk = rows // ck
    chains = [(cc, p) for cc in range(nchunk) for p in range(npair)]
    rs = lambda cc: slice(cc * ck, (cc + 1) * ck)
    ls = lambda p: slice(p * LANES, (p + 1) * LANES)
    staged = lambda which, cc, p: stg_ref[which, rs(cc), ls(p)]

    gch = [gx_ref[rs(cc), :] for cc in range(nchunk)]
    glast = [g_[ck - 1:ck, :] for g_ in gch]
    dec, kdec, gl_exp = [], [], []
    for cc, g_ in enumerate(gch):
        grow = jnp.sum(jnp.where(diag, g_, 0.0), axis=0, keepdims=True)
        dec.append(jnp.where(tri, jnp.exp(jnp.where(tri, g_ - grow, 0.0)), 0.0))
        kdec.append((stg_ref[_STG_KN, rs(cc), :].astype(F32) * jnp.exp(glast[cc] - g_)).astype(BF16))
        gl_exp.append(jnp.exp(glast[cc]))
    qd_old = [stg_ref[_STG_QD, rs(cc), :] for cc in range(nchunk)]

    seq_start = lax.rem(t, groups_per_seq) == 0
    cw = cw_ref[...]
    hsum2 = hsum_ref[0:LANES, 0:LANES]
    act, sumsq = {}, {}

    def prepare(slab):
        cs = ls(slab)
        xs_ref[0:halo, cs] = jnp.where(seq_start, 0.0, xs_ref[0:halo, cs])
        xs_ref[halo:halo + rows, cs] = x_ref[0, :, cs].astype(F32)
        xf = xs_ref[:, cs]
        y = xf * cw[CONV_WIDTH - 1:CONV_WIDTH, cs]
        for j in range(1, CONV_WIDTH):
            y = y + pltpu.roll(xf, j, axis=0) * cw[CONV_WIDTH - 1 - j:CONV_WIDTH - j, cs]
        y = y[halo:]
        xs_ref[0:halo, cs] = xf[rows:]
        y = y * _sigmoid(y)
        act[slab] = y
        if slab < 2 * npair:
            sumsq[slab] = _dot((y * y).astype(BF16), hsum2)

    def stage(p):
        qn = act[p] * (lax.rsqrt(sumsq[p] + NORM_EPS) * (HEAD_DIM ** -0.5))
        kn = act[npair + p] * lax.rsqrt(sumsq[npair + p] + NORM_EPS)
        bx, eg = bexp[:, ls(p)], jnp.exp(gexp[:, ls(p)])
        kb = kn * bx
        stg_ref[_STG_KB, :, ls(p)] = kb.astype(BF16)
        stg_ref[_STG_QN, :, ls(p)] = qn.astype(BF16)
        stg_ref[_STG_KN, :, ls(p)] = kn.astype(BF16)
        stg_ref[_STG_VB, :, ls(p)] = (act[2 * npair + p] * bx).astype(BF16)
        stg_ref[_STG_KBE, :, ls(p)] = (kb * eg).astype(BF16)
        stg_ref[_STG_QD, :, ls(p)] = (qn * eg).astype(BF16)

    todo = list(range(3 * npair))

    def prepare_some(n):
        for _ in range(n):
            if todo:
                prepare(todo.pop(0))

    kq = {(cc, p): lax.dot_general(jnp.concatenate([staged(_STG_KB, cc, p), staged(_STG_QN, cc, p)], axis=0),
                                   bd2(staged(_STG_KN, cc, p)), _NT, preferred_element_type=F32)
          for cc, p in chains}
    vbd = {(cc, p): jnp.concatenate([bd2(staged(_STG_VB, cc, p)), bd2(staged(_STG_KBE, cc, p))], axis=1)
           for cc, p in chains}
    sm = sm_ref[...]
    widen = lambda a: jnp.concatenate([a] * (rows // LANES), axis=1)
    xa = sm + widen(dtb_ref[...])
    softplus = jnp.maximum(xa, 0.0) + jnp.log(1.0 + jnp.exp(-jnp.abs(xa)))
    g = -jnp.exp(widen(alog_ref[...])) * softplus
    utri = utri_ref[...]
    gc = sum(_dot(t_, utri) for t_ in _split3(g))
    prepare_some(2)
    neg_l = {(cc, p): -jnp.where(strict[:, ls(p)], kq[cc, p][:ck] * dec[cc][:, ls(p)], 0.0) for cc, p in chains}
    qk = {(cc, p): jnp.where(tri[:, ls(p)], kq[cc, p][ck:] * dec[cc][:, ls(p)], 0.0) for cc, p in chains}
    ssum = {ch: eye2 + neg_l[ch] for ch in chains}
    pw = {ch: _dot(neg_l[ch].astype(BF16), bd2(neg_l[ch])) for ch in chains}
    gate_row = lax.broadcasted_iota(jnp.int32, (2 * DN_HEADS, rows), 0)
    comb = jnp.where(gate_row < DN_HEADS, _sigmoid(sm), gc)
    ex = lax.dot_general(jnp.concatenate(_split3(comb), axis=0), exp_ref[...], _TN, preferred_element_type=F32)
    bexp, gexp = ex[:, :dn_w], ex[:, dn_w:]
    prepare_some(2)
    span = 2
    while span * 2 < ck:
        both = {ch: _dot(jnp.concatenate([ssum[ch], pw[ch]], axis=0).astype(BF16), bd2(pw[ch])) for ch in chains}
        prepare_some(2)
        ssum = {ch: ssum[ch] + both[ch][:ck] for ch in chains}
        pw = {ch: both[ch][ck:] for ch in chains}
        span *= 2
    corr = {ch: _dot(ssum[ch].astype(BF16), bd2(pw[ch])) for ch in chains}
    prepare_some(len(todo))
    uw = {ch: _dot((ssum[ch] + corr[ch]).astype(BF16), vbd[ch]) for ch in chains}

    fresh = lax.rem(jnp.maximum(t - 1, 0), groups_per_seq) == 0
    state = [jnp.where(fresh, 0.0, st_ref[p]) for p in range(npair)]
    for cc in range(nchunk):
        wq = [_dot(jnp.concatenate([uw[cc, p][:, LANES:].astype(BF16), qd_old[cc][:, ls(p)]], axis=0),
                   state[p].astype(BF16)) for p in range(npair)]
        if cc < npair:
            stage(cc)
        vnew = [uw[cc, p][:, :LANES] - wq[p][:ck] for p in range(npair)]
        intra = [_dot(qk[cc, p].astype(BF16), bd2(vnew[p])) for p in range(npair)]
        upd = [lax.dot_general(kdec[cc][:, ls(p)], vnew[p].astype(BF16), _TN, preferred_element_type=F32)
               for p in range(npair)]
        for p in range(npair):
            o_ref[0, rs(cc), ls(p)] = (wq[p][ck:] + intra[p]).astype(o_ref.dtype)
        state = [state[p] * gl_exp[cc][:, ls(p)] + jnp.where(bmask, upd[p], 0.0) for p in range(npair)]
    for p in range(npair):
        st_ref[p] = state[p]
    for p in range(nchunk, npair):
        stage(p)
    gx_ref[...] = gexp


def _gdn(qkv_dn, small, conv_w, a_log, dt_bias, *, rows):
    b, s, w3 = qkv_dn.shape
    dn_w = DN_HEADS * HEAD_DIM
    assert w3 == 3 * dn_w and s % rows == 0 and rows % DN_CHUNK == 0
    h = DN_HEADS
    nc = s // rows
    nsteps = b * nc
    assert small.shape == (2 * h, b * s) and rows % LANES == 0
    zeros = jnp.zeros((h,), F32)
    alog_col = jnp.tile(jnp.concatenate([zeros, a_log.astype(F32)])[:, None], (1, LANES))
    dtb_col = jnp.tile(jnp.concatenate([zeros, dt_bias.astype(F32)])[:, None], (1, LANES))
    r = jnp.arange(rows)
    utri = ((r[:, None] // DN_CHUNK == r[None, :] // DN_CHUNK) & (r[:, None] <= r[None, :])).astype(BF16)
    src = jnp.arange(2 * h)[:, None]
    dst = jnp.arange(2 * dn_w)[None, :]
    expander = ((dst // dn_w == src // h) & ((dst % dn_w) // HEAD_DIM == src % h)).astype(BF16)
    expander = jnp.tile(expander, (3, 1))
    hl = jnp.arange(dn_w) // HEAD_DIM
    hsum = (hl[:, None] == hl[None, :]).astype(BF16)
    const = lambda a: pl.BlockSpec(a.shape, lambda t: (0,) * a.ndim)

    def produced(t):
        g = jnp.minimum(t, nsteps - 1)
        return g // nc, g % nc, 0

    def consumed(t):
        g = jnp.maximum(t - 1, 0)
        return g // nc, g % nc, 0

    est = (2 * (rows * w3 * 2 + 2 * h * rows * 4 + rows * dn_w * 4) + (rows + 8) * w3 * 4 + 12 * rows * w3 * 4
           + 6 * rows * dn_w * 2 + rows * dn_w * 4 + 2 * (utri.size + expander.size + hsum.size) * 2)
    return pl.pallas_call(
        functools.partial(_gdn_kernel, rows=rows, groups_per_seq=nc),
        grid=(nsteps + 1,),
        in_specs=[pl.BlockSpec((1, rows, w3), produced),
                  pl.BlockSpec((2 * h, rows), lambda t: (0, jnp.minimum(t, nsteps - 1))),
                  const(conv_w), const(alog_col), const(dtb_col), const(utri), const(expander), const(hsum)],
        out_specs=pl.BlockSpec((1, rows, dn_w), consumed),
        out_shape=jax.ShapeDtypeStruct((b, s, dn_w), F32),
        scratch_shapes=[pltpu.VMEM((rows + 8, w3), F32), pltpu.VMEM((DN_HEADS // 2, LANES, LANES), F32),
                        pltpu.VMEM((6, rows, dn_w), BF16), pltpu.VMEM((rows, dn_w), F32)],
        compiler_params=pltpu.CompilerParams(dimension_semantics=("arbitrary",),
                                             vmem_limit_bytes=_vmem_limit(est)),
        name="gdn",
    )(qkv_dn, small, conv_w.astype(F32), alog_col, dtb_col, utri, expander, hsum)


def _mixout_kernel(x_ref, ya_ref, od_ref, z_ref, gate_ref, dnw_ref, hmean_ref, wa_ref, wd_ref, wo_ref, pn_ref, o_ref):
    d = x_ref.shape[1]
    od = od_ref[...]
    ms = _dot((od * od).astype(BF16), hmean_ref[...])
    z = z_ref[...].astype(F32)
    y_dn = od * lax.rsqrt(ms + NORM_EPS) * dnw_ref[...] * (z * _sigmoid(z))
    ga = _sigmoid(gate_ref[:, :d].astype(F32))
    gd = _sigmoid(gate_ref[:, d:].astype(F32))
    merged = ga * _dot(ya_ref[...], wa_ref[...]) + gd * _dot(y_dn.astype(BF16), wd_ref[...])
    y = _dot(merged.astype(BF16), wo_ref[...])
    o_ref[...] = x_ref[...] + _rms(y, pn_ref[...])


def _mixout(x2, y_att, o_dn, z, gates, dn_norm, wa, wd, wo, post_norm, *, tm):
    t, d = x2.shape
    dn_w = DN_HEADS * HEAD_DIM
    dnw_row = jnp.tile(dn_norm.astype(F32), DN_HEADS)[None, :]
    hl = jnp.arange(dn_w) // HEAD_DIM
    hmean = ((hl[:, None] == hl[None, :]).astype(F32) / HEAD_DIM).astype(BF16)
    row = lambda w: pl.BlockSpec((tm, w), lambda i: (i, 0))
    const = lambda a: pl.BlockSpec(a.shape, lambda i: (0, 0), pipeline_mode=pl.Buffered(1))
    est = (2 * (2 * tm * d * 4 + tm * dn_w * (2 + 4 + 2) + tm * 2 * d * 2)
           + (hmean.size + wa.size + wd.size + wo.size) * 2 + 8 * tm * d * 4)
    return pl.pallas_call(
        _mixout_kernel,
        grid=(t // tm,),
        in_specs=[row(d), row(dn_w), row(dn_w), row(dn_w), row(2 * d), const(dnw_row), const(hmean),
                  const(wa), const(wd), const(wo), const(post_norm)],
        out_specs=row(d),
        out_shape=jax.ShapeDtypeStruct((t, d), F32),
        compiler_params=pltpu.CompilerParams(dimension_semantics=("arbitrary",), vmem_limit_bytes=_vmem_limit(est)),
        name="mixout",
    )(x2, y_att, o_dn, z, gates, dnw_row, hmean, wa, wd, wo, post_norm)


def _mlp_kernel(x_ref, pre_ref, w1_ref, w2_ref, post_ref, o_ref, *, ff_chunk):
    x = x_ref[...]
    hb = _rms(x, pre_ref[...]).astype(BF16)
    acc = jnp.zeros(x.shape, F32)
    for c in range(0, w1_ref.shape[1], ff_chunk):
        a = jnp.maximum(_dot(hb, w1_ref[:, c:c + ff_chunk]), 0.0)
        acc = acc + _dot((a * a).astype(BF16), w2_ref[c:c + ff_chunk, :])
    o_ref[...] = x + _rms(acc, post_ref[...])


def _mlp(x1, pre, w1, w2, post, *, tm):
    t, d = x1.shape
    row = pl.BlockSpec((tm, d), lambda i: (i, 0))
    const = lambda a: pl.BlockSpec(a.shape, lambda i: (0, 0), pipeline_mode=pl.Buffered(1))
    est = 2 * (2 * tm * d * 4) + (w1.size + w2.size) * 2 + 6 * tm * d * 4 + 2 * tm * 1024 * 4
    return pl.pallas_call(
        functools.partial(_mlp_kernel, ff_chunk=1024),
        grid=(t // tm,),
        in_specs=[row, const(pre), const(w1), const(w2), const(post)],
        out_specs=row,
        out_shape=jax.ShapeDtypeStruct((t, d), F32),
        compiler_params=pltpu.CompilerParams(dimension_semantics=("arbitrary",), vmem_limit_bytes=_vmem_limit(est)),
        name="mlp",
    )(x1, pre, w1, w2, post)


def kernel(x, pre_norm_mix, w_in, conv_w, a_log, dt_bias, dn_norm, w_branch_att, w_branch_dn,
           w_out, post_norm_mix, pre_norm_mlp, w_mlp_in, w_mlp_out, post_norm_mlp):
    b, s, d = x.shape
    att_w = ATT_HEADS * HEAD_DIM
    dn_w = DN_HEADS * HEAD_DIM
    n_main = 3 * att_w + 3 * dn_w + dn_w
    tm = 1024
    x2 = x.reshape(b * s, d)
    for l in range(w_in.shape[0]):
        wl = w_in[l]
        col_scale = jnp.where(jnp.arange(n_main) < att_w, HEAD_DIM ** -0.5 * LOG2_E, 1.0).astype(F32)
        w_main = (wl[:, :n_main] * col_scale[None, :]).astype(BF16)
        w_gate = wl[:, n_main + 2 * DN_HEADS:].astype(BF16)
        w_small = wl[:, n_main:n_main + 2 * DN_HEADS].T.astype(BF16)
        qkv_att, qkv_dn, z, gates, small = _inproj(x2, pre_norm_mix[l][None, :], w_main, w_gate, w_small, tm=tm)
        y_att = _moba(qkv_att.reshape(b, s, 3 * att_w))
        o_dn = _gdn(qkv_dn.reshape(b, s, 3 * dn_w), small, conv_w[l], a_log[l], dt_bias[l], rows=256)
        x2 = _mixout(x2, y_att.reshape(b * s, att_w), o_dn.reshape(b * s, dn_w), z, gates, dn_norm[l],
                     w_branch_att[l].astype(BF16), w_branch_dn[l].astype(BF16), w_out[l].astype(BF16),
                     post_norm_mix[l][None, :], tm=tm)
        x2 = _mlp(x2, pre_norm_mlp[l][None, :], w_mlp_in[l].astype(BF16), w_mlp_out[l].astype(BF16),
                  post_norm_mlp[l][None, :], tm=tm)
    return x2.reshape(b, s, d)
```
